```python
import math
import jax, jax.numpy as jnp
from jax import lax
import numpy as np

D_MODEL = 2048
BATCH = 2
SEQ = 16384
DEPTH = 2

S5_WIDTH = D_MODEL // 4
S5_GROUP = 16
S5_GROUPS = S5_WIDTH // S5_GROUP
S5_STATE = 64
HEAD_DIM = 64
ATTN_HEADS = 8
ATTN_KV_HEADS = 2
ATTN_WIDTH = ATTN_HEADS * HEAD_DIM
IDX_HEADS = 4
IDX_DIM = 64
TOPK_MAX = 256
Q_BLOCK = 128
SSD_WIDTH = D_MODEL // 2
SSD_HEAD_DIM = 64
SSD_HEADS = SSD_WIDTH // SSD_HEAD_DIM
SSD_GROUPS = 2
SSD_STATE = 128
SSD_CONV = 4
SSD_CHUNK = 128
SSD_XBC = SSD_WIDTH + 2 * SSD_GROUPS * SSD_STATE
MEM_LEN = 256
MEM_HEADS = 4
MEM_WIDTH = MEM_HEADS * HEAD_DIM
N_BRANCH = 4
ROPE_THETA = 500000.0
ROPE_DIM = HEAD_DIM // 4
DEEPNORM_ALPHA = (2 * DEPTH) ** 0.25
DEEPNORM_BETA = (8 * DEPTH) ** -0.25
LN_EPS = 1e-5
RMS_EPS = 1e-5

SPLITS = (
    ('s5_u', S5_WIDTH), ('s5_z', S5_WIDTH),
    ('att_q', ATTN_WIDTH), ('att_k', ATTN_KV_HEADS * HEAD_DIM), ('att_v', ATTN_KV_HEADS * HEAD_DIM), ('att_z', ATTN_WIDTH),
    ('idx_q', IDX_HEADS * IDX_DIM), ('idx_k', IDX_DIM), ('idx_w', IDX_HEADS),
    ('ssd_z', SSD_WIDTH), ('ssd_xbc', SSD_XBC), ('ssd_dt', SSD_HEADS),
    ('mem_q', MEM_WIDTH), ('mem_z', MEM_WIDTH),
    ('gates', N_BRANCH * D_MODEL),
)
IN_WIDTH = sum(w for _, w in SPLITS)

kernel_name = "hybrid_s5_dsa_ssd_deepnorm"


def split_columns(h):
    offs = np.cumsum([w for _, w in SPLITS])[:-1].tolist()
    return dict(zip([n for n, _ in SPLITS], jnp.split(h, offs, axis=-1)))


def layer_norm(x, g, b):
    xf = x.astype(jnp.float32)
    mu = jnp.mean(xf, -1, keepdims=True)
    var = jnp.mean(jnp.square(xf - mu), -1, keepdims=True)
    return ((xf - mu) * lax.rsqrt(var + LN_EPS) * g + b).astype(x.dtype)


def partial_rope(x, pos):
    half = ROPE_DIM // 2
    inv = ROPE_THETA ** (-jnp.arange(half, dtype=jnp.float32) * 2.0 / ROPE_DIM)
    ang = pos.astype(jnp.float32)[:, :, None, None] * inv
    cos, sin = jnp.cos(ang), jnp.sin(ang)
    xr = x[..., :ROPE_DIM].astype(jnp.float32)
    x1, x2 = xr[..., :half], xr[..., half:]
    rot = jnp.concatenate([x1 * cos - x2 * sin, x2 * cos + x1 * sin], -1).astype(x.dtype)
    return jnp.concatenate([rot, x[..., ROPE_DIM:]], -1)


def _complex_affine_combine(e1, e2):
    a1r, a1i, b1r, b1i = e1
    a2r, a2i, b2r, b2i = e2
    ar = a2r * a1r - a2i * a1i
    ai = a2r * a1i + a2i * a1r
    br = a2r * b1r - a2i * b1i + b2r
    bi = a2r * b1i + a2i * b1r + b2i
    return (ar, ai, br, bi)


def s5_mixer(u, lam_re, lam_im, log_dt, b_re, b_im, c_re, c_im, d_skip, w_glu, b_glu):
    bsz, L, _ = u.shape
    f32 = jnp.float32
    uf = u.reshape(bsz, L, S5_GROUPS, S5_GROUP).astype(f32)
    dt = jnp.exp(log_dt.astype(f32))[:, None]
    lr, li = lam_re.astype(f32), lam_im.astype(f32)
    mag = jnp.exp(lr * dt)
    ar, ai = mag * jnp.cos(li * dt), mag * jnp.sin(li * dt)
    den = lr * lr + li * li
    nr, ni = ar - 1.0, ai
    fr, fi = (nr * lr + ni * li) / den, (ni * lr - nr * li) / den
    bbr = fr[..., None] * b_re - fi[..., None] * b_im
    bbi = fr[..., None] * b_im + fi[..., None] * b_re
    bu_r = jnp.einsum('blgh,gph->blgp', uf, bbr.astype(f32))
    bu_i = jnp.einsum('blgh,gph->blgp', uf, bbi.astype(f32))
    a_r = jnp.broadcast_to(ar, bu_r.shape)
    a_i = jnp.broadcast_to(ai, bu_r.shape)
    _, _, s_r, s_i = lax.associative_scan(_complex_affine_combine, (a_r, a_i, bu_r, bu_i), axis=1)
    y = (jnp.einsum('blgp,ghp->blgh', s_r, c_re.astype(f32))
         - jnp.einsum('blgp,ghp->blgh', s_i, c_im.astype(f32))
         + d_skip.astype(f32) * uf)
    y = jax.nn.gelu(y.reshape(bsz, L, S5_WIDTH))
    y = y * jax.nn.sigmoid(y @ w_glu.astype(f32) + b_glu.astype(f32))
    return y.astype(u.dtype)


def dsa_attention(q, k, v, qi, ki, wi):
    bsz, L = q.shape[:2]
    f32 = jnp.float32
    topk = min(TOPK_MAX, L // 4)
    nb = L // Q_BLOCK
    grp = ATTN_HEADS // ATTN_KV_HEADS
    key_pos = jnp.arange(L)
    kif = ki.astype(f32)

    def blockify(t):
        return jnp.moveaxis(t.reshape((bsz, nb, Q_BLOCK) + t.shape[2:]), 1, 0)

    def block(args):
        qb, qib, wib, start = args
        qpos = start + jnp.arange(Q_BLOCK)
        visible = key_pos[None, :] <= qpos[:, None]
        s = jax.nn.relu(jnp.einsum('bqhd,bsd->bqhs', qib.astype(f32), kif))
        score = jnp.einsum('bqhs,bqh->bqs', s, wib.astype(f32))
        score = jnp.where(visible[None], score, -jnp.inf)
        _, sel = lax.top_k(score, topk)
        valid = sel <= qpos[None, :, None]
        kg = jax.vmap(lambda kk, ii: kk[ii])(k, sel)
        vg = jax.vmap(lambda vv, ii: vv[ii])(v, sel)
        qg = qb.reshape(bsz, Q_BLOCK, ATTN_KV_HEADS, grp, HEAD_DIM)
        logits = jnp.einsum('bqgjd,bqkgd->bqgjk', qg, kg).astype(f32) * (HEAD_DIM ** -0.5)
        logits = jnp.where(valid[:, :, None, None, :], logits, -jnp.inf)
        p = jax.nn.softmax(logits, axis=-1).astype(vg.dtype)
        o = jnp.einsum('bqgjk,bqkgd->bqgjd', p, vg)
        return o.reshape(bsz, Q_BLOCK, ATTN_WIDTH)

    starts = jnp.arange(nb) * Q_BLOCK
    out = lax.map(block, (blockify(q), blockify(qi), blockify(wi), starts))
    return jnp.moveaxis(out, 0, 1).reshape(bsz, L, ATTN_WIDTH)


def causal_depthwise_conv(x, w, b):
    y = lax.conv_general_dilated(x, w[:, None, :].astype(x.dtype), window_strides=(1,),
                                 padding=[(SSD_CONV - 1, 0)], dimension_numbers=('NWC', 'WIO', 'NWC'),
                                 feature_group_count=x.shape[-1])
    return y + b


def ssd_mixer(z, xbc, dt_raw, conv_w, conv_b, dt_bias, a_log, d_skip, norm_g):
    bsz, L, _ = xbc.shape
    f32 = jnp.float32
    hpg = SSD_HEADS // SSD_GROUPS
    nc = L // SSD_CHUNK
    xbc = jax.nn.silu(causal_depthwise_conv(xbc, conv_w, conv_b))
    xs, bm, cm = jnp.split(xbc, [SSD_WIDTH, SSD_WIDTH + SSD_GROUPS * SSD_STATE], axis=-1)
    x5 = xs.reshape(bsz, nc, SSD_CHUNK, SSD_GROUPS, hpg, SSD_HEAD_DIM).astype(f32)
    bm = bm.reshape(bsz, nc, SSD_CHUNK, SSD_GROUPS, SSD_STATE).astype(f32)
    cm = cm.reshape(bsz, nc, SSD_CHUNK, SSD_GROUPS, SSD_STATE).astype(f32)
    dt = jax.nn.softplus(dt_raw.astype(f32) + dt_bias.astype(f32))
    dt = dt.reshape(bsz, nc, SSD_CHUNK, SSD_GROUPS, hpg)
    a = -jnp.exp(a_log.astype(f32)).reshape(SSD_GROUPS, hpg)
    acs = jnp.cumsum(jnp.moveaxis(dt * a, 2, -1), axis=-1)
    xdt = x5 * dt[..., None]
    idx = jnp.arange(SSD_CHUNK)
    causal = idx[:, None] >= idx[None, :]
    seg = acs[..., :, None] - acs[..., None, :]
    lmat = jnp.exp(jnp.where(causal, seg, -jnp.inf))
    cb = jnp.einsum('bclgn,bcsgn->bcgls', cm, bm)
    y_diag = jnp.einsum('bcgls,bcgjls,bcsgjp->bclgjp', cb, lmat, xdt)
    decay_states = jnp.exp(acs[..., -1:] - acs)
    states = jnp.einsum('bclgn,bcgjl,bclgjp->bcgjpn', bm, decay_states, xdt)
    chunk_decay = jnp.exp(acs[..., -1])

    def step(h, inp):
        st, dec = inp
        return h * dec[..., None, None] + st, h

    h0 = jnp.zeros_like(states[:, 0])
    _, prev = lax.scan(step, h0, (jnp.moveaxis(states, 1, 0), jnp.moveaxis(chunk_decay, 1, 0)))
    prev = jnp.moveaxis(prev, 0, 1)
    y_off = jnp.einsum('bclgn,bcgjpn,bcgjl->bclgjp', cm, prev, jnp.exp(acs))
    y = y_diag + y_off + x5 * d_skip.astype(f32).reshape(SSD_GROUPS, hpg)[..., None]
    y = y.reshape(bsz, L, SSD_WIDTH) * jax.nn.silu(z.astype(f32))
    yg = y.reshape(bsz, L, SSD_GROUPS, SSD_WIDTH // SSD_GROUPS)
    yg = yg * lax.rsqrt(jnp.mean(yg * yg, -1, keepdims=True) + RMS_EPS)
    return (yg.reshape(bsz, L, SSD_WIDTH) * norm_g.astype(f32)).astype(z.dtype)


def memory_attention(q, k, v):
    logits = jnp.einsum('blhd,bmhd->bhlm', q, k).astype(jnp.float32) * (HEAD_DIM ** -0.5)
    p = jax.nn.softmax(logits, axis=-1).astype(v.dtype)
    return jnp.einsum('bhlm,bmhd->blhd', p, v)


def nrm(k, shape, scale):
    return jax.random.normal(k, shape, jnp.float32) * scale


def setup_inputs(seed: int = 0) -> dict:
    key = jax.random.key(seed)
    ks = list(jax.random.split(key, 32))
    nd = DEPTH
    G, P, H = S5_GROUPS, S5_STATE, S5_GROUP
    x = nrm(ks[0], (BATCH, SEQ, D_MODEL), 1.0)
    mem = nrm(ks[1], (BATCH, MEM_LEN, D_MODEL), 1.0)
    positions = (jax.random.randint(ks[2], (BATCH, 1), 0, 4096) + jnp.arange(SEQ)[None, :]).astype(jnp.int32)
    w_in = nrm(ks[3], (nd, D_MODEL, IN_WIDTH), D_MODEL ** -0.5)
    b_gate = nrm(ks[4], (nd, N_BRANCH * D_MODEL), 0.02)
    s5_lam_re = -0.5 + nrm(ks[5], (nd, G, P), 0.01)
    s5_lam_im = math.pi * jnp.arange(P, dtype=jnp.float32) + nrm(ks[6], (nd, G, P), 0.01)
    s5_log_dt = jax.random.uniform(ks[7], (nd, G), jnp.float32, math.log(1e-3), math.log(1e-1))
    s5_b_re = nrm(ks[8], (nd, G, P, H), (2 * H) ** -0.5)
    s5_b_im = nrm(ks[9], (nd, G, P, H), (2 * H) ** -0.5)
    s5_c_re = nrm(ks[10], (nd, G, H, P), (2 * P) ** -0.5)
    s5_c_im = nrm(ks[11], (nd, G, H, P), (2 * P) ** -0.5)
    s5_d = nrm(ks[12], (nd, G, H), 1.0)
    s5_w_glu = nrm(ks[13], (nd, S5_WIDTH, S5_WIDTH), S5_WIDTH ** -0.5)
    s5_b_glu = nrm(ks[14], (nd, S5_WIDTH), 0.02)
    ssd_conv_w = nrm(ks[15], (nd, SSD_CONV, SSD_XBC), SSD_CONV ** -0.5)
    ssd_conv_b = nrm(ks[16], (nd, SSD_XBC), 0.02)
    dt0 = jnp.exp(jax.random.uniform(ks[17], (nd, SSD_HEADS), jnp.float32, math.log(1e-3), math.log(1e-1)))
    ssd_dt_bias = dt0 + jnp.log(-jnp.expm1(-dt0))
    ssd_a_log = jnp.log(jax.random.uniform(ks[18], (nd, SSD_HEADS), jnp.float32, 1.0, 16.0))
    ssd_d = 1.0 + nrm(ks[19], (nd, SSD_HEADS), 0.1)
    ssd_norm_g = 1.0 + nrm(ks[20], (nd, SSD_WIDTH), 0.02)
    mem_w_kv = nrm(ks[21], (nd, D_MODEL, 2 * MEM_WIDTH), D_MODEL ** -0.5)
    w_br_s5 = nrm(ks[22], (nd, S5_WIDTH, D_MODEL), S5_WIDTH ** -0.5)
    w_br_attn = nrm(ks[23], (nd, ATTN_WIDTH, D_MODEL), ATTN_WIDTH ** -0.5)
    w_br_ssd = nrm(ks[24], (nd, SSD_WIDTH, D_MODEL), SSD_WIDTH ** -0.5)
    w_br_mem = nrm(ks[25], (nd, MEM_WIDTH, D_MODEL), MEM_WIDTH ** -0.5)
    w_out = nrm(ks[26], (nd, D_MODEL, D_MODEL), DEEPNORM_BETA * D_MODEL ** -0.5)
    ln_g = 1.0 + nrm(ks[27], (nd, D_MODEL), 0.02)
    ln_b = nrm(ks[28], (nd, D_MODEL), 0.02)
    return {"x": x, "mem": mem, "positions": positions, "w_in": w_in, "b_gate": b_gate,
            "s5_lam_re": s5_lam_re, "s5_lam_im": s5_lam_im, "s5_log_dt": s5_log_dt,
            "s5_b_re": s5_b_re, "s5_b_im": s5_b_im, "s5_c_re": s5_c_re, "s5_c_im": s5_c_im,
            "s5_d": s5_d, "s5_w_glu": s5_w_glu, "s5_b_glu": s5_b_glu,
            "ssd_conv_w": ssd_conv_w, "ssd_conv_b": ssd_conv_b, "ssd_dt_bias": ssd_dt_bias,
            "ssd_a_log": ssd_a_log, "ssd_d": ssd_d, "ssd_norm_g": ssd_norm_g,
            "mem_w_kv": mem_w_kv, "w_br_s5": w_br_s5, "w_br_attn": w_br_attn,
            "w_br_ssd": w_br_ssd, "w_br_mem": w_br_mem, "w_out": w_out,
            "ln_g": ln_g, "ln_b": ln_b}


def reference(x, mem, positions, w_in, b_gate, s5_lam_re, s5_lam_im, s5_log_dt, s5_b_re, s5_b_im,
              s5_c_re, s5_c_im, s5_d, s5_w_glu, s5_b_glu, ssd_conv_w, ssd_conv_b, ssd_dt_bias,
              ssd_a_log, ssd_d, ssd_norm_g, mem_w_kv, w_br_s5, w_br_attn, w_br_ssd, w_br_mem,
              w_out, ln_g, ln_b):
    bsz, L, _ = x.shape
    idx_scale = (IDX_HEADS ** -0.5) * (IDX_DIM ** -0.5)
    for i in range(DEPTH):
        p = split_columns(x @ w_in[i])
        y_s5 = s5_mixer(p['s5_u'], s5_lam_re[i], s5_lam_im[i], s5_log_dt[i], s5_b_re[i], s5_b_im[i],
                        s5_c_re[i], s5_c_im[i], s5_d[i], s5_w_glu[i], s5_b_glu[i])
        y_s5 = y_s5 * jax.nn.silu(p['s5_z'])
        q = partial_rope(p['att_q'].reshape(bsz, L, ATTN_HEADS, HEAD_DIM), positions)
        k = partial_rope(p['att_k'].reshape(bsz, L, ATTN_KV_HEADS, HEAD_DIM), positions)
        v = p['att_v'].reshape(bsz, L, ATTN_KV_HEADS, HEAD_DIM)
        qi = partial_rope(p['idx_q'].reshape(bsz, L, IDX_HEADS, IDX_DIM), positions)
        ki = partial_rope(p['idx_k'].reshape(bsz, L, 1, IDX_DIM), positions)[:, :, 0]
        wi = p['idx_w'] * idx_scale
        y_att = dsa_attention(q, k, v, qi, ki, wi) * jax.nn.silu(p['att_z'])
        y_ssd = ssd_mixer(p['ssd_z'], p['ssd_xbc'], p['ssd_dt'], ssd_conv_w[i], ssd_conv_b[i],
                          ssd_dt_bias[i], ssd_a_log[i], ssd_d[i], ssd_norm_g[i])
        mk, mv = jnp.split(mem @ mem_w_kv[i], 2, axis=-1)
        y_mem = memory_attention(p['mem_q'].reshape(bsz, L, MEM_HEADS, HEAD_DIM),
                                 mk.reshape(bsz, MEM_LEN, MEM_HEADS, HEAD_DIM),
                                 mv.reshape(bsz, MEM_LEN, MEM_HEADS, HEAD_DIM)).reshape(bsz, L, MEM_WIDTH)
        y_mem = y_mem * jax.nn.silu(p['mem_z'])
        g = jax.nn.sigmoid(p['gates'] + b_gate[i]).reshape(bsz, L, N_BRANCH, D_MODEL)
        merged = (g[:, :, 0] * (y_s5 @ w_br_s5[i]) + g[:, :, 1] * (y_att @ w_br_attn[i])
                  + g[:, :, 2] * (y_ssd @ w_br_ssd[i]) + g[:, :, 3] * (y_mem @ w_br_mem[i]))
        out = merged @ w_out[i]
        x = layer_norm(DEEPNORM_ALPHA * x + out, ln_g[i], ln_b[i])
    return x
```

```python
import functools
import math

import jax
import jax.numpy as jnp
from jax import lax
from jax.experimental import pallas as pl
from jax.experimental.pallas import tpu as pltpu

F32 = jnp.float32
BF16 = jnp.bfloat16

D_MODEL = 2048
S5_WIDTH = 512
S5_GROUP = 16
S5_GROUPS = 32
S5_STATE = 64
HEAD_DIM = 64
ATTN_HEADS = 8
ATTN_KV_HEADS = 2
ATTN_WIDTH = 512
IDX_HEADS = 4
IDX_DIM = 64
TOPK_MAX = 256
SSD_WIDTH = 1024
SSD_HEAD_DIM = 64
SSD_HEADS = 16
SSD_GROUPS = 2
SSD_STATE = 128
SSD_CONV = 4
SSD_CHUNK = 128
SSD_XBC = 1536
MEM_LEN = 256
MEM_HEADS = 4
MEM_WIDTH = 256
N_BRANCH = 4
ROPE_THETA = 500000.0
ROPE_DIM = 16
DEPTH = 2
DEEPNORM_ALPHA = (2 * DEPTH) ** 0.25
LN_EPS = 1e-5
RMS_EPS = 1e-5

SPLITS = (
    ('s5_u', 512), ('s5_z', 512),
    ('att_q', 512), ('att_k', 128), ('att_v', 128), ('att_z', 512),
    ('idx_q', 256), ('idx_k', 64), ('idx_w', 4),
    ('ssd_z', 1024), ('ssd_xbc', 1536), ('ssd_dt', 16),
    ('mem_q', 256), ('mem_z', 256),
    ('gates', 8192),
)

PACKED = (
    ('ssd_z', 1024), ('s5_u', 512), ('ssd_xbc', 1536), ('s5_z', 512), ('att_q', 512), ('att_z', 512),
    ('idx_q', 256), ('mem_q', 256), ('mem_z', 256), ('att_k', 128), ('att_v', 128),
    ('idx_kw', 128), ('ssd_dt', 128), ('pad', 256),
)
PACKED_WIDTH = sum(w for _, w in PACKED)


def _packed_block(name):
    off = 0
    for n, w in PACKED:
        if n == name:
            assert off % w == 0
            return w, off // w
        off += w
    raise KeyError(name)


VMEM_LIMIT = 56 * 1024 * 1024


def _cparams(n_axes, vmem=VMEM_LIMIT):
    return pltpu.CompilerParams(dimension_semantics=("arbitrary",) * n_axes, vmem_limit_bytes=vmem)


def _silu(x):
    return x * (1.0 / (1.0 + jnp.exp(-x)))


def _sigmoid(x):
    return 1.0 / (1.0 + jnp.exp(-x))


def _mm_kernel(x_ref, w_ref, o_ref, xb_ref):
    @pl.when(pl.program_id(1) == 0)
    def _():
        xb_ref[...] = x_ref[...].astype(BF16)

    o_ref[...] = jnp.dot(xb_ref[...], w_ref[...], preferred_element_type=F32).astype(o_ref.dtype)


def _matmul(x, w, tm, tn, out_dtype=F32):
    m, k = x.shape
    n = w.shape[1]
    tm = min(tm, m)
    tn = min(tn, n)
    assert m % tm == 0 and n % tn == 0
    return pl.pallas_call(
        _mm_kernel,
        grid=(m // tm, n // tn),
        in_specs=[pl.BlockSpec((tm, k), lambda i, j: (i, 0)),
                  pl.BlockSpec((k, tn), lambda i, j: (0, j))],
        out_specs=pl.BlockSpec((tm, tn), lambda i, j: (i, j)),
        out_shape=jax.ShapeDtypeStruct((m, n), out_dtype),
        scratch_shapes=[pltpu.VMEM((tm, k), BF16)],
        compiler_params=_cparams(2),
    )(x, w)


def _rope128(x, c, sp, sm):
    return x * c + pltpu.roll(x, 8, 1) * sp + pltpu.roll(x, 120, 1) * sm


def _prep_kernel(q_ref, iq_ref, k_ref, v_ref, kw_ref, c_ref, sp_ref, sm_ref,
                 qo_ref, iqo_ref, ko_ref, vo_ref, kio_ref, wo_ref):
    c, sp, sm = c_ref[...], sp_ref[...], sm_ref[...]
    t = q_ref.shape[0]
    zeros64 = jnp.zeros((t, 64), F32)
    scale = HEAD_DIM ** -0.5
    for pair in range(ATTN_HEADS // 2):
        r = _rope128(q_ref[:, pair * 128:(pair + 1) * 128], c, sp, sm) * scale
        for sub in range(2):
            h = pair * 2 + sub
            part = r[:, sub * 64:(sub + 1) * 64]
            if h // (ATTN_HEADS // ATTN_KV_HEADS) == 0:
                full = jnp.concatenate([part, zeros64], axis=1)
            else:
                full = jnp.concatenate([zeros64, part], axis=1)
            qo_ref[h] = full.astype(BF16)
    for pair in range(IDX_HEADS // 2):
        r = _rope128(iq_ref[:, pair * 128:(pair + 1) * 128], c, sp, sm)
        for sub in range(2):
            iqo_ref[pair * 2 + sub] = r[:, sub * 64:(sub + 1) * 64].astype(BF16)
    ko_ref[...] = _rope128(k_ref[...], c, sp, sm).astype(BF16)
    vo_ref[...] = v_ref[...].astype(BF16)
    kw = kw_ref[...]
    kio_ref[...] = _rope128(kw, c, sp, sm)[:, :64].astype(BF16)
    wo_ref[...] = kw[:, 64:64 + IDX_HEADS] * ((IDX_HEADS ** -0.5) * (IDX_DIM ** -0.5))


def _prep(h, ctab, sptab, smtab, t=512):
    n = h.shape[0]
    t = min(t, n)

    def hspec(name):
        w, idx = _packed_block(name)
        return pl.BlockSpec((t, w), lambda i: (i, idx))

    tspec = pl.BlockSpec((t, 128), lambda i: (i, 0))
    return pl.pallas_call(
        _prep_kernel,
        grid=(n // t,),
        in_specs=[hspec('att_q'), hspec('idx_q'), hspec('att_k'), hspec('att_v'), hspec('idx_kw'),
                  tspec, tspec, tspec],
        out_specs=[pl.BlockSpec((ATTN_HEADS, t, 128), lambda i: (0, i, 0)),
                   pl.BlockSpec((IDX_HEADS, t, 64), lambda i: (0, i, 0)),
                   pl.BlockSpec((t, 128), lambda i: (i, 0)),
                   pl.BlockSpec((t, 128), lambda i: (i, 0)),
                   pl.BlockSpec((t, 64), lambda i: (i, 0)),
                   pl.BlockSpec((t, IDX_HEADS), lambda i: (i, 0))],
        out_shape=[jax.ShapeDtypeStruct((ATTN_HEADS, n, 128), BF16),
                   jax.ShapeDtypeStruct((IDX_HEADS, n, 64), BF16),
                   jax.ShapeDtypeStruct((n, 128), BF16),
                   jax.ShapeDtypeStruct((n, 128), BF16),
                   jax.ShapeDtypeStruct((n, 64), BF16),
                   jax.ShapeDtypeStruct((n, IDX_HEADS), F32)],
        compiler_params=_cparams(1),
    )(h, h, h, h, h, ctab, sptab, smtab)


BISECT_STEPS = 20
NT_DIMS = (((1,), (1,)), ((), ()))


def _dsa_kernel(q_ref, iq_ref, wt_ref, az_ref, ki_ref, k_ref, vt_ref, o_ref,
                s_ref, acc_ref, m_ref, l_ref, *, qb, topk):
    kb = qb
    j = pl.program_id(1)
    q0 = j * qb
    nk = j + 1
    neg = -jnp.inf
    qpos = q0 + lax.broadcasted_iota(jnp.int32, (1, qb), 1)
    row = lax.broadcasted_iota(jnp.int32, (kb, qb), 0)
    wt = wt_ref[...]

    def chunk(c):
        return pl.ds(pl.multiple_of(c * kb, kb), kb)

    def p1(c, carry):
        mx, mn = carry
        kic = ki_ref[chunk(c), :]
        acc = jnp.zeros((kb, qb), F32)
        for h in range(IDX_HEADS):
            d = lax.dot_general(kic, iq_ref[h], NT_DIMS, preferred_element_type=F32)
            acc = acc + jnp.maximum(d, 0.0) * wt[h:h + 1, :]
        vis = (row + c * kb) <= qpos
        s_ref[chunk(c), :] = jnp.where(vis, acc, neg)
        mx = jnp.maximum(mx, jnp.max(jnp.where(vis, acc, neg), axis=0, keepdims=True))
        mn = jnp.minimum(mn, jnp.min(jnp.where(vis, acc, jnp.inf), axis=0, keepdims=True))
        return mx, mn

    mx, mn = lax.fori_loop(0, nk, p1, (jnp.full((1, qb), neg, F32), jnp.full((1, qb), jnp.inf, F32)))

    def fold(x):
        return x.reshape(kb // 8, 8, qb).sum(axis=0)

    def count(pred_fn):
        def body(c, a):
            return a + fold(jnp.where(pred_fn(s_ref[chunk(c), :]), 1.0, 0.0))
        a = lax.fori_loop(0, nk, body, jnp.zeros((8, qb), F32))
        return jnp.sum(a, axis=0, keepdims=True)

    def max_below(hi):
        def body(c, a):
            s = s_ref[chunk(c), :]
            return jnp.maximum(a, jnp.where(s < hi, s, neg).reshape(kb // 8, 8, qb).max(axis=0))
        a = lax.fori_loop(0, nk, body, jnp.full((8, qb), neg, F32))
        return jnp.max(a, axis=0, keepdims=True)

    nvis = (qpos + 1).astype(F32)
    keff = jnp.minimum(nvis, float(topk))

    def bis(_, carry):
        lo, hi = carry
        top = jnp.minimum(hi, mx)
        mid = lo + 0.5 * (top - lo)
        ok = count(lambda s: s >= mid) >= keff
        return jnp.where(ok, mid, lo), jnp.where(ok, hi, mid)

    lo, hi = lax.fori_loop(0, BISECT_STEPS, bis, (mn, jnp.full((1, qb), jnp.inf, F32)))

    def fin_cond(carry):
        return carry[0] > 0.0

    def fin_body(carry):
        _, hi, thr, done = carry
        cand = max_below(hi)
        ok = count(lambda s: s >= cand) >= keff
        newly = jnp.logical_and(ok, done < 0.5)
        thr = jnp.where(newly, cand, thr)
        hi = jnp.where(jnp.logical_or(ok, done > 0.5), hi, cand)
        done = jnp.where(ok, 1.0, done)
        return jnp.sum(1.0 - done), hi, thr, done

    done0 = jnp.where(nvis <= float(topk), 1.0, 0.0)
    _, _, thr, _ = lax.while_loop(fin_cond, fin_body, (jnp.sum(1.0 - done0), hi, mn, done0))
    need = keff - count(lambda s: s > thr)

    m_ref[...] = jnp.full(m_ref.shape, neg, F32)
    l_ref[...] = jnp.zeros(l_ref.shape, F32)
    acc_ref[...] = jnp.zeros(acc_ref.shape, F32)
    tril = jnp.where(lax.broadcasted_iota(jnp.int32, (kb, kb), 1) <= lax.broadcasted_iota(jnp.int32, (kb, kb), 0),
                     1.0, 0.0).astype(BF16)

    def p3(c, run):
        s = s_ref[chunk(c), :]
        eq = jnp.where(s == thr, 1.0, 0.0)
        rank = jnp.dot(tril, eq.astype(BF16), preferred_element_type=F32) + run
        tie_ok = jnp.where(s == thr, rank, jnp.inf) <= need
        bias = jnp.where(s > thr, 0.0, jnp.where(tie_ok, 0.0, neg))
        kc = k_ref[chunk(c), :]
        for h in range(ATTN_HEADS):
            g = h // (ATTN_HEADS // ATTN_KV_HEADS)
            lg = lax.dot_general(kc, q_ref[h], NT_DIMS, preferred_element_type=F32) + bias
            m_old = m_ref[h:h + 1, :]
            m_new = jnp.maximum(m_old, jnp.max(lg, axis=0, keepdims=True))
            m_safe = jnp.where(m_new == neg, 0.0, m_new)
            p = jnp.exp(lg - m_safe)
            alpha = jnp.exp(m_old - m_safe)
            l_ref[h:h + 1, :] = alpha * l_ref[h:h + 1, :] + jnp.sum(p, axis=0, keepdims=True)
            m_ref[h:h + 1, :] = m_new
            pv = jnp.dot(vt_ref[c, g * 64:(g + 1) * 64, :], p.astype(BF16), preferred_element_type=F32)
            acc_ref[h * 64:(h + 1) * 64, :] = alpha * acc_ref[h * 64:(h + 1) * 64, :] + pv
        return rank[kb - 1:kb, :]

    lax.fori_loop(0, nk, p3, jnp.zeros((1, qb), F32))

    for h in range(ATTN_HEADS):
        acc_ref[h * 64:(h + 1) * 64, :] = acc_ref[h * 64:(h + 1) * 64, :] * (1.0 / l_ref[h:h + 1, :])
    o_ref[...] = (acc_ref[...].T * _silu(az_ref[...])).astype(o_ref.dtype)


def _dsa(h, qp, iqp, wt, ki, kr, vt4, bsz, seq, qb):
    n = bsz * seq
    nq = seq // qb
    topk = min(TOPK_MAX, seq // 4)
    zw, zidx = _packed_block('att_z')
    kern = functools.partial(_dsa_kernel, qb=qb, topk=topk)
    return pl.pallas_call(
        kern,
        grid=(bsz, nq),
        in_specs=[pl.BlockSpec((ATTN_HEADS, qb, 128), lambda b, j: (0, b * nq + j, 0)),
                  pl.BlockSpec((IDX_HEADS, qb, 64), lambda b, j: (0, b * nq + j, 0)),
                  pl.BlockSpec((None, IDX_HEADS, qb), lambda b, j: (b, 0, j)),
                  pl.BlockSpec((qb, zw), lambda b, j: (b * nq + j, zidx)),
                  pl.BlockSpec((seq, 64), lambda b, j: (b, 0)),
                  pl.BlockSpec((seq, 128), lambda b, j: (b, 0)),
                  pl.BlockSpec((None, nq, 128, qb), lambda b, j: (b, 0, 0, 0))],
        out_specs=pl.BlockSpec((qb, ATTN_WIDTH), lambda b, j: (b * nq + j, 0)),
        out_shape=jax.ShapeDtypeStruct((n, ATTN_WIDTH), BF16),
        scratch_shapes=[pltpu.VMEM((seq, qb), F32),
                        pltpu.VMEM((ATTN_WIDTH, qb), F32),
                        pltpu.VMEM((ATTN_HEADS, qb), F32),
                        pltpu.VMEM((ATTN_HEADS, qb), F32)],
        compiler_params=_cparams(2),
    )(qp, iqp, wt, h, ki, kr, vt4)


def _ssd_kernel(z_ref, xbc_ref, dt_ref, cw_ref, cb_ref, dtb_ref, a_ref, d_ref, ng_ref, o_ref,
                xc_ref, st_ref, y_ref):
    c = pl.program_id(1)
    L = SSD_CHUNK

    @pl.when(c == 0)
    def _():
        xc_ref[0:8, :] = jnp.zeros((8, SSD_XBC), F32)
        st_ref[...] = jnp.zeros(st_ref.shape, F32)

    xc_ref[8:8 + L, :] = xbc_ref[...]
    conv = cb_ref[...] + cw_ref[3:4, :] * xc_ref[8:8 + L, :]
    for k in range(1, SSD_CONV):
        conv = conv + cw_ref[3 - k:4 - k, :] * xc_ref[8 - k:8 - k + L, :]
    xc_ref[0:8, :] = xc_ref[L:L + 8, :]
    xbc = _silu(conv)

    dtr = dt_ref[...] + dtb_ref[...]
    dt = jnp.maximum(dtr, 0.0) + jnp.log1p(jnp.exp(-jnp.abs(dtr)))
    da = dt * a_ref[...]
    ri = lax.broadcasted_iota(jnp.int32, (L, L), 0)
    ci = lax.broadcasted_iota(jnp.int32, (L, L), 1)
    causal = ci <= ri
    tri = jnp.where(causal, 1.0, 0.0)
    acs = jnp.dot(tri, da, preferred_element_type=F32, precision=lax.Precision.HIGHEST)
    acs_t = acs.T
    e_acs = jnp.exp(acs)
    e_last = jnp.exp(acs[L - 1:L, :])
    dec = jnp.exp(acs[L - 1:L, :] - acs)

    hpg = SSD_HEADS // SSD_GROUPS
    for g in range(SSD_GROUPS):
        bm = xbc[:, SSD_WIDTH + g * SSD_STATE:SSD_WIDTH + (g + 1) * SSD_STATE]
        cm = xbc[:, SSD_WIDTH + (SSD_GROUPS + g) * SSD_STATE:SSD_WIDTH + (SSD_GROUPS + g + 1) * SSD_STATE]
        bmb = bm.astype(BF16)
        cmb = cm.astype(BF16)
        cb = lax.dot_general(cmb, bmb, NT_DIMS, preferred_element_type=F32)
        bmt = bm.T.astype(BF16)
        for jj in range(hpg):
            hd = g * hpg + jj
            xj = xbc[:, hd * SSD_HEAD_DIM:(hd + 1) * SSD_HEAD_DIM]
            xdt = xj * dt[:, hd:hd + 1]
            seg = acs[:, hd:hd + 1] - acs_t[hd:hd + 1, :]
            lm = jnp.exp(jnp.where(causal, seg, -jnp.inf))
            y = jnp.dot((cb * lm).astype(BF16), xdt.astype(BF16), preferred_element_type=F32)
            prev = st_ref[hd]
            y = y + jnp.dot(cmb, prev.astype(BF16), preferred_element_type=F32) * e_acs[:, hd:hd + 1]
            st = jnp.dot(bmt, (xdt * dec[:, hd:hd + 1]).astype(BF16), preferred_element_type=F32)
            st_ref[hd] = prev * e_last[:, hd:hd + 1] + st
            y_ref[:, hd * SSD_HEAD_DIM:(hd + 1) * SSD_HEAD_DIM] = y + xj * d_ref[:, hd * SSD_HEAD_DIM:(hd + 1) * SSD_HEAD_DIM]

    y = y_ref[...] * _silu(z_ref[...])
    gw = SSD_WIDTH // SSD_GROUPS
    parts = []
    for g in range(SSD_GROUPS):
        yg = y[:, g * gw:(g + 1) * gw]
        parts.append(yg * lax.rsqrt(jnp.mean(yg * yg, axis=-1, keepdims=True) + RMS_EPS))
    o_ref[...] = (jnp.concatenate(parts, axis=1) * ng_ref[...]).astype(o_ref.dtype)


def _ssd(h, conv_w, conv_b, dt_bias, a_log, d_skip, norm_g, bsz, seq):
    n = bsz * seq
    nc = seq // SSD_CHUNK
    L = SSD_CHUNK

    def hspec(name):
        w, idx = _packed_block(name)
        return pl.BlockSpec((L, w), lambda b, c: (b * nc + c, idx))

    def full(shape):
        return pl.BlockSpec(shape, lambda b, c: (0,) * len(shape))

    pad = 128 - SSD_HEADS
    dtb = jnp.pad(dt_bias.astype(F32), (0, pad)).reshape(1, 128)
    a = jnp.pad(-jnp.exp(a_log.astype(F32)), (0, pad)).reshape(1, 128)
    dexp = jnp.repeat(d_skip.astype(F32), SSD_HEAD_DIM).reshape(1, SSD_WIDTH)
    return pl.pallas_call(
        _ssd_kernel,
        grid=(bsz, nc),
        in_specs=[hspec('ssd_z'), hspec('ssd_xbc'), hspec('ssd_dt'),
                  full((SSD_CONV, SSD_XBC)), full((1, SSD_XBC)), full((1, 128)), full((1, 128)),
                  full((1, SSD_WIDTH)), full((1, SSD_WIDTH))],
        out_specs=pl.BlockSpec((L, SSD_WIDTH), lambda b, c: (b * nc + c, 0)),
        out_shape=jax.ShapeDtypeStruct((n, SSD_WIDTH), BF16),
        scratch_shapes=[pltpu.VMEM((L + 8, SSD_XBC), F32),
                        pltpu.VMEM((SSD_HEADS, SSD_STATE, SSD_HEAD_DIM), F32),
                        pltpu.VMEM((L, SSD_WIDTH), F32)],
        compiler_params=_cparams(2),
    )(h, h, h, conv_w.astype(F32), conv_b.astype(F32).reshape(1, SSD_XBC), dtb, a, dexp,
      norm_g.astype(F32).reshape(1, SSD_WIDTH))


S5_T = 16
S5_CW = S5_T * S5_GROUP
S5_PAIRS = S5_GROUPS // 2


def _s5_tables(lam_re, lam_im, log_dt, b_re, b_im, c_re, c_im, d_skip):
    f32 = F32
    hp = lax.Precision.HIGHEST
    dt = jnp.exp(log_dt.astype(f32))[:, None]
    lr, li = lam_re.astype(f32), lam_im.astype(f32)
    mag = jnp.exp(lr * dt)
    ar, ai = mag * jnp.cos(li * dt), mag * jnp.sin(li * dt)
    den = lr * lr + li * li
    nr, ni = ar - 1.0, ai
    fr, fi = (nr * lr + ni * li) / den, (ni * lr - nr * li) / den
    bbr = fr[..., None] * b_re - fi[..., None] * b_im
    bbi = fr[..., None] * b_im + fi[..., None] * b_re
    taus = jnp.arange(S5_T + 1, dtype=f32)[:, None, None]
    pmag = jnp.exp(taus * (lr * dt)[None])
    pr, pi = pmag * jnp.cos(taus * (li * dt)[None]), pmag * jnp.sin(taus * (li * dt)[None])
    lbr = pr[..., None] * bbr[None] - pi[..., None] * bbi[None]
    lbi = pr[..., None] * bbi[None] + pi[..., None] * bbr[None]
    cr, ci = c_re.astype(f32), c_im.astype(f32)
    kmat = (jnp.einsum('ghp,tgpk->tghk', cr, lbr, precision=hp)
            - jnp.einsum('ghp,tgpk->tghk', ci, lbi, precision=hp))
    t_idx = jnp.arange(S5_T)
    diff = t_idx[None, :] - t_idx[:, None]
    kk = kmat[jnp.clip(diff, 0, S5_T)]
    kk = jnp.where((diff >= 0)[:, :, None, None, None], kk, 0.0)
    mt = jnp.transpose(kk, (2, 0, 4, 1, 3)).reshape(S5_GROUPS, S5_CW, S5_CW)
    wr = jnp.transpose(lbr[S5_T - 1 - t_idx], (1, 0, 3, 2)).reshape(S5_GROUPS, S5_CW, S5_STATE)
    wi = jnp.transpose(lbi[S5_T - 1 - t_idx], (1, 0, 3, 2)).reshape(S5_GROUPS, S5_CW, S5_STATE)
    pr1, pi1 = pr[1:], pi[1:]
    vr = cr[None] * pr1[:, :, None, :] - ci[None] * pi1[:, :, None, :]
    vi = cr[None] * pi1[:, :, None, :] + ci[None] * pr1[:, :, None, :]
    v_r = jnp.transpose(vr, (1, 3, 0, 2)).reshape(S5_GROUPS, S5_STATE, S5_CW)
    v_i = -jnp.transpose(vi, (1, 3, 0, 2)).reshape(S5_GROUPS, S5_STATE, S5_CW)
    z = jnp.zeros
    wr2, wi2 = wr.reshape(S5_PAIRS, 2, S5_CW, S5_STATE), wi.reshape(S5_PAIRS, 2, S5_CW, S5_STATE)
    zz = z((S5_PAIRS, S5_CW, S5_STATE), f32)
    wp = jnp.concatenate([jnp.concatenate([wr2[:, 0], zz, wi2[:, 0], zz], 2),
                          jnp.concatenate([zz, wr2[:, 1], zz, wi2[:, 1]], 2)], 1)
    vr2, vi2 = v_r.reshape(S5_PAIRS, 2, S5_STATE, S5_CW), v_i.reshape(S5_PAIRS, 2, S5_STATE, S5_CW)
    zv = z((S5_PAIRS, S5_STATE, S5_CW), f32)
    vp = jnp.concatenate([jnp.concatenate([vr2[:, 0], zv], 2), jnp.concatenate([zv, vr2[:, 1]], 2),
                          jnp.concatenate([vi2[:, 0], zv], 2), jnp.concatenate([zv, vi2[:, 1]], 2)], 1)
    lam_t = (pr[S5_T].reshape(1, S5_GROUPS * S5_STATE), pi[S5_T].reshape(1, S5_GROUPS * S5_STATE))
    dexp = jnp.tile(d_skip.astype(f32)[:, None, :], (1, S5_T, 1)).reshape(S5_GROUPS, 1, S5_CW)
    return mt.astype(BF16), wp.astype(BF16), vp.astype(BF16), lam_t[0], lam_t[1], dexp


def _s5_kernel(u_ref, mt_ref, w_ref, v_ref, ar_ref, ai_ref, d_ref, o_ref,
               zr_ref, zi_ref, sr_ref, si_ref, cr_ref, ci_ref, *, rows):
    @pl.when(pl.program_id(1) == 0)
    def _():
        cr_ref[...] = jnp.zeros(cr_ref.shape, F32)
        ci_ref[...] = jnp.zeros(ci_ref.shape, F32)

    for p in range(S5_PAIRS):
        ub = jnp.concatenate([u_ref[2 * p], u_ref[2 * p + 1]], axis=1).astype(BF16)
        zz = jnp.dot(ub, w_ref[p], preferred_element_type=F32)
        zr_ref[:, p * 128:(p + 1) * 128] = zz[:, :128]
        zi_ref[:, p * 128:(p + 1) * 128] = zz[:, 128:]

    ar, ai = ar_ref[...], ai_ref[...]

    def step(r, carry):
        sr, si = carry
        sr_ref[pl.ds(r, 1), :] = sr
        si_ref[pl.ds(r, 1), :] = si
        zr = zr_ref[pl.ds(r, 1), :]
        zi = zi_ref[pl.ds(r, 1), :]
        return ar * sr - ai * si + zr, ar * si + ai * sr + zi

    sr, si = lax.fori_loop(0, rows, step, (cr_ref[...], ci_ref[...]))
    cr_ref[...] = sr
    ci_ref[...] = si

    for p in range(S5_PAIRS):
        u2 = jnp.concatenate([u_ref[2 * p], u_ref[2 * p + 1]], axis=1)
        ub = u2.astype(BF16)
        y = jnp.concatenate([jnp.dot(ub[:, :S5_CW], mt_ref[2 * p], preferred_element_type=F32),
                             jnp.dot(ub[:, S5_CW:], mt_ref[2 * p + 1], preferred_element_type=F32)], axis=1)
        sin = jnp.concatenate([sr_ref[:, p * 128:(p + 1) * 128], si_ref[:, p * 128:(p + 1) * 128]], axis=1)
        y = y + jnp.dot(sin.astype(BF16), v_ref[p], preferred_element_type=F32)
        y = y + u2 * jnp.concatenate([d_ref[2 * p], d_ref[2 * p + 1]], axis=1)
        y = jax.nn.gelu(y)
        o_ref[2 * p] = y[:, :S5_CW]
        o_ref[2 * p + 1] = y[:, S5_CW:]


def _s5(u4, tables, bsz, nch, rows):
    mtp, wp, vp, lar, lai, dexp = tables
    rows = min(rows, nch)
    nt = nch // rows
    gs = S5_GROUPS * S5_STATE

    def full(shape):
        return pl.BlockSpec(shape, lambda b, i: (0,) * len(shape))

    kern = functools.partial(_s5_kernel, rows=rows)
    return pl.pallas_call(
        kern,
        grid=(bsz, nt),
        in_specs=[pl.BlockSpec((None, S5_GROUPS, rows, S5_CW), lambda b, i: (b, 0, i, 0)),
                  full(mtp.shape), full(wp.shape), full(vp.shape), full((1, gs)), full((1, gs)),
                  full(dexp.shape)],
        out_specs=pl.BlockSpec((None, S5_GROUPS, rows, S5_CW), lambda b, i: (b, 0, i, 0)),
        out_shape=jax.ShapeDtypeStruct((bsz, S5_GROUPS, nch, S5_CW), F32),
        scratch_shapes=[pltpu.VMEM((rows, gs), F32), pltpu.VMEM((rows, gs), F32),
                        pltpu.VMEM((rows, gs), F32), pltpu.VMEM((rows, gs), F32),
                        pltpu.VMEM((1, gs), F32), pltpu.VMEM((1, gs), F32)],
        compiler_params=_cparams(2),
    )(u4, mtp, wp, vp, lar, lai, dexp)


def _mem_kernel(q_ref, z_ref, mk_ref, mv_ref, o_ref):
    scale = HEAD_DIM ** -0.5
    mk = mk_ref[...].astype(BF16)
    mv = mv_ref[...].astype(BF16)
    outs = []
    for h in range(MEM_HEADS):
        sl = slice(h * HEAD_DIM, (h + 1) * HEAD_DIM)
        qh = (q_ref[:, sl] * scale).astype(BF16)
        lg = lax.dot_general(qh, mk[:, sl], NT_DIMS, preferred_element_type=F32)
        m = jnp.max(lg, axis=-1, keepdims=True)
        p = jnp.exp(lg - m)
        p = p * (1.0 / jnp.sum(p, axis=-1, keepdims=True))
        outs.append(jnp.dot(p.astype(BF16), mv[:, sl], preferred_element_type=F32))
    o_ref[...] = (jnp.concatenate(outs, axis=1) * _silu(z_ref[...])).astype(o_ref.dtype)


def _mem_attn(h, memkv, bsz, seq, t=512):
    n = bsz * seq
    t = min(t, seq)
    nt = seq // t
    qw, qidx = _packed_block('mem_q')
    zw, zidx = _packed_block('mem_z')
    return pl.pallas_call(
        _mem_kernel,
        grid=(bsz, nt),
        in_specs=[pl.BlockSpec((t, qw), lambda b, i: (b * nt + i, qidx)),
                  pl.BlockSpec((t, zw), lambda b, i: (b * nt + i, zidx)),
                  pl.BlockSpec((MEM_LEN, MEM_WIDTH), lambda b, i: (b, 0)),
                  pl.BlockSpec((MEM_LEN, MEM_WIDTH), lambda b, i: (b, 1))],
        out_specs=pl.BlockSpec((t, MEM_WIDTH), lambda b, i: (b * nt + i, 0)),
        out_shape=jax.ShapeDtypeStruct((n, MEM_WIDTH), BF16),
        compiler_params=_cparams(2),
    )(h, h, memkv, memkv)


def _merge_kernel(x_ref, s5_ref, s5z_ref, att_ref, ssd_ref, mem_ref,
                  wglu_ref, bglu_ref, wg_ref, bg_ref, ws5_ref, watt_ref, wssd_ref, wmem_ref,
                  wout_ref, lng_ref, lnb_ref, o_ref, xb_ref, ys5_ref, acc_ref):
    nidx = pl.program_id(1)

    @pl.when(nidx == 0)
    def _():
        xb_ref[...] = x_ref[...].astype(BF16)
        y = s5_ref[...]
        glu = y * _sigmoid(jnp.dot(y.astype(BF16), wglu_ref[...], preferred_element_type=F32) + bglu_ref[...])
        ys5_ref[...] = (glu * _silu(s5z_ref[...])).astype(BF16)
        acc_ref[...] = jnp.zeros(acc_ref.shape, F32)

    xb = xb_ref[...]
    branches = ((ys5_ref, ws5_ref), (att_ref, watt_ref), (ssd_ref, wssd_ref), (mem_ref, wmem_ref))
    merged = None
    for i, (y_ref, w_ref) in enumerate(branches):
        gate = _sigmoid(jnp.dot(xb, wg_ref[i], preferred_element_type=F32) + bg_ref[i])
        term = gate * jnp.dot(y_ref[...], w_ref[...], preferred_element_type=F32)
        merged = term if merged is None else merged + term
    acc_ref[...] += jnp.dot(merged.astype(BF16), wout_ref[...], preferred_element_type=F32)

    @pl.when(nidx == pl.num_programs(1) - 1)
    def _():
        r = DEEPNORM_ALPHA * x_ref[...] + acc_ref[...]
        mu = jnp.mean(r, axis=-1, keepdims=True)
        rc = r - mu
        var = jnp.mean(rc * rc, axis=-1, keepdims=True)
        o_ref[...] = rc * lax.rsqrt(var + LN_EPS) * lng_ref[...] + lnb_ref[...]


def _merge(x, h, ys5, yatt, yssd, ymem, wglu, bglu, wg4, bg4, ws5, watt, wssd, wmem, wout, lng, lnb,
           tm=512, tn=256):
    n = x.shape[0]
    tm = min(tm, n)
    zw, zidx = _packed_block('s5_z')
    row = lambda w: pl.BlockSpec((tm, w), lambda i, c: (i, 0))
    colw = lambda k: pl.BlockSpec((k, tn), lambda i, c: (0, c))
    const = lambda shape: pl.BlockSpec(shape, lambda i, c: (0,) * len(shape))
    return pl.pallas_call(
        _merge_kernel,
        grid=(n // tm, D_MODEL // tn),
        in_specs=[row(D_MODEL), row(S5_WIDTH), pl.BlockSpec((tm, zw), lambda i, c: (i, zidx)),
                  row(ATTN_WIDTH), row(SSD_WIDTH), row(MEM_WIDTH),
                  const((S5_WIDTH, S5_WIDTH)), const((1, S5_WIDTH)),
                  pl.BlockSpec((N_BRANCH, D_MODEL, tn), lambda i, c: (0, 0, c)),
                  pl.BlockSpec((N_BRANCH, 1, tn), lambda i, c: (0, 0, c)),
                  colw(S5_WIDTH), colw(ATTN_WIDTH), colw(SSD_WIDTH), colw(MEM_WIDTH),
                  pl.BlockSpec((tn, D_MODEL), lambda i, c: (c, 0)),
                  const((1, D_MODEL)), const((1, D_MODEL))],
        out_specs=pl.BlockSpec((tm, D_MODEL), lambda i, c: (i, 0)),
        out_shape=jax.ShapeDtypeStruct((n, D_MODEL), F32),
        scratch_shapes=[pltpu.VMEM((tm, D_MODEL), BF16), pltpu.VMEM((tm, S5_WIDTH), BF16),
                        pltpu.VMEM((tm, D_MODEL), F32)],
        compiler_params=_cparams(2),
    )(x, ys5, h, yatt, yssd, ymem, wglu, bglu, wg4, bg4, ws5, watt, wssd, wmem, wout, lng, lnb)


def _split_w_in(w):
    cols, off = {}, 0
    for name, width in SPLITS:
        cols[name] = w[:, off:off + width]
        off += width
    d = w.shape[0]
    pieces = []
    for name, width in PACKED:
        if name == 'idx_kw':
            pieces += [cols['idx_k'], cols['idx_w'], jnp.zeros((d, width - IDX_DIM - IDX_HEADS), w.dtype)]
        elif name == 'ssd_dt':
            pieces += [cols['ssd_dt'], jnp.zeros((d, width - SSD_HEADS), w.dtype)]
        elif name == 'pad':
            pieces.append(jnp.zeros((d, width), w.dtype))
        else:
            pieces.append(cols[name])
    packed = jnp.concatenate(pieces, axis=1).astype(BF16)
    gates = cols['gates'].reshape(d, N_BRANCH, D_MODEL).transpose(1, 0, 2).astype(BF16)
    return packed, gates


def _rope_tables(positions):
    half = ROPE_DIM // 2
    inv = ROPE_THETA ** (-jnp.arange(half, dtype=F32) * 2.0 / ROPE_DIM)
    ang = positions.astype(F32).reshape(-1)[:, None] * inv
    cos, sin = jnp.cos(ang), jnp.sin(ang)
    n = ang.shape[0]
    ones = jnp.ones((n, HEAD_DIM - ROPE_DIM), F32)
    zeros = jnp.zeros((n, HEAD_DIM - ROPE_DIM), F32)
    zh = jnp.zeros((n, half), F32)
    c64 = jnp.concatenate([cos, cos, ones], axis=1)
    sp64 = jnp.concatenate([zh, sin, zeros], axis=1)
    sm64 = jnp.concatenate([-sin, zh, zeros], axis=1)
    dup = lambda t: jnp.concatenate([t, t], axis=1)
    return dup(c64), dup(sp64), dup(sm64)


def kernel(x, mem, positions, w_in, b_gate, s5_lam_re, s5_lam_im, s5_log_dt, s5_b_re, s5_b_im, s5_c_re, s5_c_im, s5_d, s5_w_glu, s5_b_glu, ssd_conv_w, ssd_conv_b, ssd_dt_bias, ssd_a_log, ssd_d, ssd_norm_g, mem_w_kv, w_br_s5, w_br_attn, w_br_ssd, w_br_mem, w_out, ln_g, ln_b):
    bsz, seq, d = x.shape
    n = bsz * seq
    depth = w_in.shape[0]
    qb = min(256, seq)
    nq = seq // qb
    nch = seq // S5_T
    ctab, sptab, smtab = _rope_tables(positions)
    xf = x.reshape(n, d).astype(F32)
    memf = mem.reshape(bsz * MEM_LEN, d).astype(F32)
    s5w, s5idx = _packed_block('s5_u')

    for i in range(depth):
        w_packed, w_gates = _split_w_in(w_in[i])
        h = _matmul(xf, w_packed, 1024, 512)
        memkv = _matmul(memf, mem_w_kv[i].astype(BF16), 512, 512)

        u4 = (h[:, s5idx * s5w:(s5idx + 1) * s5w]
              .reshape(bsz, nch, S5_T, S5_GROUPS, S5_GROUP).transpose(0, 3, 1, 2, 4)
              .reshape(bsz, S5_GROUPS, nch, S5_CW))
        tables = _s5_tables(s5_lam_re[i], s5_lam_im[i], s5_log_dt[i], s5_b_re[i], s5_b_im[i],
                            s5_c_re[i], s5_c_im[i], s5_d[i])
        y4 = _s5(u4, tables, bsz, nch, 128)
        ys5 = (y4.reshape(bsz, S5_GROUPS, nch, S5_T, S5_GROUP).transpose(0, 2, 3, 1, 4)
               .reshape(n, S5_WIDTH))

        qp, iqp, kr, vb, ki, wi = _prep(h, ctab, sptab, smtab)
        wt = wi.reshape(bsz, seq, IDX_HEADS).transpose(0, 2, 1)
        vt4 = vb.reshape(bsz, nq, qb, 128).transpose(0, 1, 3, 2)
        yatt = _dsa(h, qp, iqp, wt, ki, kr, vt4, bsz, seq, qb)

        yssd = _ssd(h, ssd_conv_w[i], ssd_conv_b[i], ssd_dt_bias[i], ssd_a_log[i], ssd_d[i],
                    ssd_norm_g[i], bsz, seq)

        ymem = _mem_attn(h, memkv, bsz, seq)

        xf = _merge(xf, h, ys5, yatt, yssd, ymem,
                    s5_w_glu[i].astype(BF16), s5_b_glu[i].astype(F32).reshape(1, S5_WIDTH),
                    w_gates, b_gate[i].astype(F32).reshape(N_BRANCH, 1, D_MODEL),
                    w_br_s5[i].astype(BF16), w_br_attn[i].astype(BF16), w_br_ssd[i].astype(BF16),
                    w_br_mem[i].astype(BF16), w_out[i].astype(BF16),
                    ln_g[i].astype(F32).reshape(1, D_MODEL), ln_b[i].astype(F32).reshape(1, D_MODEL))
    return xf.reshape(bsz, seq, d).astype(x.dtype)
```

```python
import functools
import math

import jax
import jax.numpy as jnp
from jax import lax
from jax.experimental import pallas as pl
from jax.experimental.pallas import tpu as pltpu

F32 = jnp.float32
BF16 = jnp.bfloat16

D_MODEL = 2048
S5_WIDTH = 512
S5_GROUP = 16
S5_GROUPS = 32
S5_STATE = 64
HEAD_DIM = 64
ATTN_HEADS = 8
ATTN_KV_HEADS = 2
ATTN_WIDTH = 512
IDX_HEADS = 4
IDX_DIM = 64
TOPK_MAX = 256
SSD_WIDTH = 1024
SSD_HEAD_DIM = 64
SSD_HEADS = 16
SSD_GROUPS = 2
SSD_STATE = 128
SSD_CONV = 4
SSD_CHUNK = 128
SSD_XBC = 1536
MEM_LEN = 256
MEM_HEADS = 4
MEM_WIDTH = 256
N_BRANCH = 4
ROPE_THETA = 500000.0
ROPE_DIM = 16
DEPTH = 2
DEEPNORM_ALPHA = (2 * DEPTH) ** 0.25
LN_EPS = 1e-5
RMS_EPS = 1e-5

SPLITS = (
    ('s5_u', 512), ('s5_z', 512),
    ('att_q', 512), ('att_k', 128), ('att_v', 128), ('att_z', 512),
    ('idx_q', 256), ('idx_k', 64), ('idx_w', 4),
    ('ssd_z', 1024), ('ssd_xbc', 1536), ('ssd_dt', 16),
    ('mem_q', 256), ('mem_z', 256),
    ('gates', 8192),
)

PACKED = (
    ('ssd_z', 1024), ('s5_u', 512), ('ssd_xbc', 1536), ('s5_z', 512), ('att_q', 512), ('att_z', 512),
    ('idx_q', 256), ('mem_q', 256), ('mem_z', 256), ('att_k', 128), ('att_v', 128),
    ('idx_kw', 128), ('ssd_dt', 128), ('pad', 256),
)
PACKED_WIDTH = sum(w for _, w in PACKED)


def _packed_block(name):
    off = 0
    for n, w in PACKED:
        if n == name:
            assert off % w == 0
            return w, off // w
        off += w
    raise KeyError(name)


VMEM_LIMIT = 56 * 1024 * 1024


def _cparams(n_axes, vmem=VMEM_LIMIT):
    return pltpu.CompilerParams(dimension_semantics=("arbitrary",) * n_axes, vmem_limit_bytes=vmem)


def _silu(x):
    return x * (1.0 / (1.0 + jnp.exp(-x)))


def _sigmoid(x):
    return 1.0 / (1.0 + jnp.exp(-x))


def _mm_kernel(x_ref, w_ref, o_ref, xb_ref):
    @pl.when(pl.program_id(1) == 0)
    def _():
        xb_ref[...] = x_ref[...].astype(BF16)

    o_ref[...] = jnp.dot(xb_ref[...], w_ref[...], preferred_element_type=F32).astype(o_ref.dtype)


def _matmul(x, w, tm, tn, out_dtype=F32):
    m, k = x.shape
    n = w.shape[1]
    tm = min(tm, m)
    tn = min(tn, n)
    assert m % tm == 0 and n % tn == 0
    return pl.pallas_call(
        _mm_kernel,
        grid=(m // tm, n // tn),
        in_specs=[pl.BlockSpec((tm, k), lambda i, j: (i, 0)),
                  pl.BlockSpec((k, tn), lambda i, j: (0, j))],
        out_specs=pl.BlockSpec((tm, tn), lambda i, j: (i, j)),
        out_shape=jax.ShapeDtypeStruct((m, n), out_dtype),
        scratch_shapes=[pltpu.VMEM((tm, k), BF16)],
        compiler_params=_cparams(2),
    )(x, w)


def _rope128(x, c, sp, sm):
    return x * c + pltpu.roll(x, 8, 1) * sp + pltpu.roll(x, 120, 1) * sm


def _prep_kernel(q_ref, iq_ref, k_ref, v_ref, kw_ref, c_ref, sp_ref, sm_ref,
                 qo_ref, iqo_ref, ko_ref, vo_ref, kio_ref, wo_ref):
    c, sp, sm = c_ref[...], sp_ref[...], sm_ref[...]
    t = q_ref.shape[0]
    zeros64 = jnp.zeros((t, 64), F32)
    scale = HEAD_DIM ** -0.5 * math.log2(math.e)
    for pair in range(ATTN_HEADS // 2):
        r = _rope128(q_ref[:, pair * 128:(pair + 1) * 128], c, sp, sm) * scale
        for sub in range(2):
            h = pair * 2 + sub
            part = r[:, sub * 64:(sub + 1) * 64]
            if h // (ATTN_HEADS // ATTN_KV_HEADS) == 0:
                full = jnp.concatenate([part, zeros64], axis=1)
            else:
                full = jnp.concatenate([zeros64, part], axis=1)
            qo_ref[h] = full.astype(BF16)
    for pair in range(IDX_HEADS // 2):
        r = _rope128(iq_ref[:, pair * 128:(pair + 1) * 128], c, sp, sm)
        for sub in range(2):
            iqo_ref[pair * 2 + sub] = r[:, sub * 64:(sub + 1) * 64].astype(BF16)
    ko_ref[...] = _rope128(k_ref[...], c, sp, sm).astype(BF16)
    vo_ref[...] = v_ref[...].astype(BF16)
    kw = kw_ref[...]
    kio_ref[...] = _rope128(kw, c, sp, sm)[:, :64].astype(BF16)
    wo_ref[...] = kw[:, 64:64 + IDX_HEADS] * ((IDX_HEADS ** -0.5) * (IDX_DIM ** -0.5))


def _prep(h, ctab, sptab, smtab, t=512):
    n = h.shape[0]
    t = min(t, n)

    def hspec(name):
        w, idx = _packed_block(name)
        return pl.BlockSpec((t, w), lambda i: (i, idx))

    tspec = pl.BlockSpec((t, 128), lambda i: (i, 0))
    return pl.pallas_call(
        _prep_kernel,
        grid=(n // t,),
        in_specs=[hspec('att_q'), hspec('idx_q'), hspec('att_k'), hspec('att_v'), hspec('idx_kw'),
                  tspec, tspec, tspec],
        out_specs=[pl.BlockSpec((ATTN_HEADS, t, 128), lambda i: (0, i, 0)),
                   pl.BlockSpec((IDX_HEADS, t, 64), lambda i: (0, i, 0)),
                   pl.BlockSpec((t, 128), lambda i: (i, 0)),
                   pl.BlockSpec((t, 128), lambda i: (i, 0)),
                   pl.BlockSpec((t, 64), lambda i: (i, 0)),
                   pl.BlockSpec((t, IDX_HEADS), lambda i: (i, 0))],
        out_shape=[jax.ShapeDtypeStruct((ATTN_HEADS, n, 128), BF16),
                   jax.ShapeDtypeStruct((IDX_HEADS, n, 64), BF16),
                   jax.ShapeDtypeStruct((n, 128), BF16),
                   jax.ShapeDtypeStruct((n, 128), BF16),
                   jax.ShapeDtypeStruct((n, 64), BF16),
                   jax.ShapeDtypeStruct((n, IDX_HEADS), F32)],
        compiler_params=_cparams(1),
    )(h, h, h, h, h, ctab, sptab, smtab)


BISECT_STEPS = 15
V_ROWS = 80
NT_DIMS = (((1,), (1,)), ((), ()))


def _dsa_kernel(q_ref, iq_ref, wt_ref, az_ref, ki_ref, k_ref, vt_ref, o_ref,
                s_ref, acc_ref, l_ref, lg_ref, *, qb, topk):
    kb = qb
    j = pl.program_id(1)
    q0 = j * qb
    nk = j + 1
    neg = -jnp.inf
    qpos = q0 + lax.broadcasted_iota(jnp.int32, (1, qb), 1)
    row = lax.broadcasted_iota(jnp.int32, (kb, qb), 0)
    wt = wt_ref[...]

    def chunk(c):
        if isinstance(c, int):
            return pl.ds(c * kb, kb)
        return pl.ds(pl.multiple_of(c * kb, kb), kb)

    def p1(c, carry):
        mx, mn = carry
        kic = ki_ref[chunk(c), :]
        acc = jnp.zeros((kb, qb), F32)
        for h in range(IDX_HEADS):
            d = lax.dot_general(kic, iq_ref[h], NT_DIMS, preferred_element_type=F32)
            acc = acc + jnp.maximum(d, 0.0) * wt[h:h + 1, :]
        vis = (row + c * kb) <= qpos
        s_ref[chunk(c), :] = jnp.where(vis, acc, neg)
        mx = jnp.maximum(mx, jnp.max(jnp.where(vis, acc, neg), axis=0, keepdims=True))
        mn = jnp.minimum(mn, jnp.min(jnp.where(vis, acc, jnp.inf), axis=0, keepdims=True))
        return mx, mn

    mx, mn = lax.fori_loop(0, nk, p1, (jnp.full((1, qb), neg, F32), jnp.full((1, qb), jnp.inf, F32)))

    @pl.when(nk % 2 == 1)
    def _():
        s_ref[chunk(nk), :] = jnp.full((kb, qb), neg, F32)

    npair = (nk + 1) // 2

    sub = 64

    def sweep(fn, init):
        def body(i, a):
            base = pl.multiple_of(i * (2 * kb), 2 * kb)
            for r in range(0, 2 * kb, sub):
                a = fn(a, s_ref[pl.ds(base + r, sub), :])
            return a
        return lax.fori_loop(0, npair, body, init)

    def count(pred_fn):
        a = sweep(lambda a, s: a + jnp.where(pred_fn(s), 1.0, 0.0).reshape(sub // 8, 8, qb).sum(axis=0),
                  jnp.zeros((8, qb), F32))
        return jnp.sum(a, axis=0, keepdims=True)

    def max_below(hi):
        a = sweep(lambda a, s: jnp.maximum(a, jnp.where(s < hi, s, neg).reshape(sub // 8, 8, qb).max(axis=0)),
                  jnp.full((8, qb), neg, F32))
        return jnp.max(a, axis=0, keepdims=True)

    nvis = (qpos + 1).astype(F32)
    keff = jnp.minimum(nvis, float(topk))

    def bis_cond(carry):
        return jnp.logical_and(carry[0] > 0.0, carry[1] < BISECT_STEPS)

    def bis_body(carry):
        _, it, lo, hi, clo = carry
        top = jnp.minimum(hi, mx)
        mid = lo + 0.5 * (top - lo)
        cnt = count(lambda s: s >= mid)
        ok = cnt >= keff
        lo = jnp.where(ok, mid, lo)
        clo = jnp.where(ok, cnt, clo)
        hi = jnp.where(ok, hi, mid)
        return jnp.sum(jnp.where(clo == keff, 0.0, 1.0)), it + 1, lo, hi, clo

    _, _, lo, hi, clo = lax.while_loop(
        bis_cond, bis_body,
        (jnp.sum(jnp.where(nvis == keff, 0.0, 1.0)), jnp.int32(0), mn, jnp.full((1, qb), jnp.inf, F32), nvis))

    def fin_cond(carry):
        return carry[0] > 0.0

    def fin_body(carry):
        _, hi, thr, done = carry
        cand = max_below(hi)
        ok = count(lambda s: s >= cand) >= keff
        newly = jnp.logical_and(ok, done < 0.5)
        thr = jnp.where(newly, cand, thr)
        hi = jnp.where(jnp.logical_or(ok, done > 0.5), hi, cand)
        done = jnp.where(ok, 1.0, done)
        return jnp.sum(1.0 - done), hi, thr, done

    done0 = jnp.where(clo == keff, 1.0, 0.0)
    _, _, thr, _ = lax.while_loop(fin_cond, fin_body, (jnp.sum(1.0 - done0), hi, lo, done0))
    need = keff - count(lambda s: s > thr)

    l_ref[...] = jnp.zeros(l_ref.shape, F32)
    acc_ref[...] = jnp.zeros(acc_ref.shape, F32)
    tril = jnp.where(lax.broadcasted_iota(jnp.int32, (kb, kb), 1) <= lax.broadcasted_iota(jnp.int32, (kb, kb), 0),
                     1.0, 0.0).astype(BF16)

    def stage1(c, run, m_old):
        s = s_ref[chunk(c), :]
        eq = jnp.where(s == thr, 1.0, 0.0)
        rank = jnp.dot(tril, eq.astype(BF16), preferred_element_type=F32) + run
        tie_ok = jnp.where(s == thr, rank, jnp.inf) <= need
        bias = jnp.where(s > thr, 0.0, jnp.where(tie_ok, 0.0, neg))
        kc = k_ref[chunk(c), :]
        slot = c % 2
        cms = []
        for h in range(ATTN_HEADS):
            lg = lax.dot_general(kc, q_ref[h], NT_DIMS, preferred_element_type=F32) + bias
            lg_ref[slot, h] = lg
            cms.append(jnp.max(lg.reshape(kb // 8, 8, qb).max(axis=0), axis=0, keepdims=True))
        m_new = jnp.maximum(m_old, jnp.concatenate(cms, axis=0))
        m_safe = jnp.where(m_new == neg, 0.0, m_new)
        alpha = jnp.exp2(m_old - m_safe)
        return rank[kb - 1:kb, :], m_new, m_safe, alpha

    def stage2(c, m_safe, alpha):
        slot = c % 2
        lsum = []
        for h in range(ATTN_HEADS):
            g = h // (ATTN_HEADS // ATTN_KV_HEADS)
            p = jnp.exp2(lg_ref[slot, h] - m_safe[h:h + 1, :])
            pv = jnp.dot(vt_ref[c, g], p.astype(BF16), preferred_element_type=F32)
            lsum.append(pv[HEAD_DIM:HEAD_DIM + 1, :])
            acc_ref[h * 64:(h + 1) * 64, :] = alpha[h:h + 1, :] * acc_ref[h * 64:(h + 1) * 64, :] + pv[:HEAD_DIM, :]
        l_ref[...] = alpha * l_ref[...] + jnp.concatenate(lsum, axis=0)

    def p3(c, carry):
        run, m_old, m_safe, alpha = carry
        stage2(c - 1, m_safe, alpha)
        return stage1(c, run, m_old)

    carry = stage1(0, jnp.zeros((1, qb), F32), jnp.full((ATTN_HEADS, qb), neg, F32))
    _, _, m_safe, alpha = lax.fori_loop(1, nk, p3, carry)
    stage2(nk - 1, m_safe, alpha)

    for h in range(ATTN_HEADS):
        acc_ref[h * 64:(h + 1) * 64, :] = acc_ref[h * 64:(h + 1) * 64, :] * (1.0 / l_ref[h:h + 1, :])
    o_ref[...] = (acc_ref[...].T * _silu(az_ref[...])).astype(o_ref.dtype)


def _value_layout(vb, bsz, nq, qb):
    vt = vb.reshape(bsz, nq, qb, ATTN_KV_HEADS, HEAD_DIM).transpose(0, 1, 3, 4, 2)
    ones = jnp.ones((bsz, nq, ATTN_KV_HEADS, 1, qb), vb.dtype)
    zeros = jnp.zeros((bsz, nq, ATTN_KV_HEADS, V_ROWS - HEAD_DIM - 1, qb), vb.dtype)
    return jnp.concatenate([vt, ones, zeros], axis=3)


def _dsa(h, qp, iqp, wt, ki, kr, vt4, bsz, seq, qb):
    n = bsz * seq
    nq = seq // qb
    topk = min(TOPK_MAX, seq // 4)
    zw, zidx = _packed_block('att_z')
    kern = functools.partial(_dsa_kernel, qb=qb, topk=topk)
    return pl.pallas_call(
        kern,
        grid=(bsz, nq),
        in_specs=[pl.BlockSpec((ATTN_HEADS, qb, 128), lambda b, j: (0, b * nq + j, 0)),
                  pl.BlockSpec((IDX_HEADS, qb, 64), lambda b, j: (0, b * nq + j, 0)),
                  pl.BlockSpec((None, IDX_HEADS, qb), lambda b, j: (b, 0, j)),
                  pl.BlockSpec((qb, zw), lambda b, j: (b * nq + j, zidx)),
                  pl.BlockSpec((seq, 64), lambda b, j: (b, 0)),
                  pl.BlockSpec((seq, 128), lambda b, j: (b, 0)),
                  pl.BlockSpec((None, nq, ATTN_KV_HEADS, V_ROWS, qb), lambda b, j: (b, 0, 0, 0, 0))],
        out_specs=pl.BlockSpec((qb, ATTN_WIDTH), lambda b, j: (b * nq + j, 0)),
        out_shape=jax.ShapeDtypeStruct((n, ATTN_WIDTH), BF16),
        scratch_shapes=[pltpu.VMEM((seq + qb, qb), F32),
                        pltpu.VMEM((ATTN_WIDTH, qb), F32),
                        pltpu.VMEM((ATTN_HEADS, qb), F32),
                        pltpu.VMEM((2, ATTN_HEADS, qb, qb), F32)],
        compiler_params=_cparams(2),
    )(qp, iqp, wt, h, ki, kr, vt4)


def _ssd_kernel(z_ref, xbc_ref, dt_ref, cw_ref, cb_ref, dtb_ref, a_ref, d_ref, ng_ref, o_ref,
                xc_ref, st_ref, y_ref):
    c = pl.program_id(1)
    L = SSD_CHUNK

    @pl.when(c == 0)
    def _():
        xc_ref[0:8, :] = jnp.zeros((8, SSD_XBC), F32)
        st_ref[...] = jnp.zeros(st_ref.shape, F32)

    xc_ref[8:8 + L, :] = xbc_ref[...]
    conv = cb_ref[...] + cw_ref[3:4, :] * xc_ref[8:8 + L, :]
    for k in range(1, SSD_CONV):
        conv = conv + cw_ref[3 - k:4 - k, :] * xc_ref[8 - k:8 - k + L, :]
    xc_ref[0:8, :] = xc_ref[L:L + 8, :]
    xbc = _silu(conv)

    dtr = dt_ref[...] + dtb_ref[...]
    dt = jnp.maximum(dtr, 0.0) + jnp.log1p(jnp.exp(-jnp.abs(dtr)))
    da = dt * a_ref[...]
    ri = lax.broadcasted_iota(jnp.int32, (L, L), 0)
    ci = lax.broadcasted_iota(jnp.int32, (L, L), 1)
    causal = ci <= ri
    tri = jnp.where(causal, 1.0, 0.0)
    acs = jnp.dot(tri, da, preferred_element_type=F32, precision=lax.Precision.HIGHEST)
    acs_t = acs.T
    e_acs = jnp.exp(acs)
    e_last = jnp.exp(acs[L - 1:L, :])
    dec = jnp.exp(acs[L - 1:L, :] - acs)

    hpg = SSD_HEADS // SSD_GROUPS
    for g in range(SSD_GROUPS):
        bm = xbc[:, SSD_WIDTH + g * SSD_STATE:SSD_WIDTH + (g + 1) * SSD_STATE]
        cm = xbc[:, SSD_WIDTH + (SSD_GROUPS + g) * SSD_STATE:SSD_WIDTH + (SSD_GROUPS + g + 1) * SSD_STATE]
        bmb = bm.astype(BF16)
        cmb = cm.astype(BF16)
        cb = lax.dot_general(cmb, bmb, NT_DIMS, preferred_element_type=F32)
        bmt = bm.T.astype(BF16)
        for jj in range(hpg):
            hd = g * hpg + jj
            xj = xbc[:, hd * SSD_HEAD_DIM:(hd + 1) * SSD_HEAD_DIM]
            xdt = xj * dt[:, hd:hd + 1]
            seg = acs[:, hd:hd + 1] - acs_t[hd:hd + 1, :]
            lm = jnp.exp(jnp.where(causal, seg, -jnp.inf))
            y = jnp.dot((cb * lm).astype(BF16), xdt.astype(BF16), preferred_element_type=F32)
            prev = st_ref[hd]
            y = y + jnp.dot(cmb, prev.astype(BF16), preferred_element_type=F32) * e_acs[:, hd:hd + 1]
            st = jnp.dot(bmt, (xdt * dec[:, hd:hd + 1]).astype(BF16), preferred_element_type=F32)
            st_ref[hd] = prev * e_last[:, hd:hd + 1] + st
            y_ref[:, hd * SSD_HEAD_DIM:(hd + 1) * SSD_HEAD_DIM] = y + xj * d_ref[:, hd * SSD_HEAD_DIM:(hd + 1) * SSD_HEAD_DIM]

    y = y_ref[...] * _silu(z_ref[...])
    gw = SSD_WIDTH // SSD_GROUPS
    parts = []
    for g in range(SSD_GROUPS):
        yg = y[:, g * gw:(g + 1) * gw]
        parts.append(yg * lax.rsqrt(jnp.mean(yg * yg, axis=-1, keepdims=True) + RMS_EPS))
    o_ref[...] = (jnp.concatenate(parts, axis=1) * ng_ref[...]).astype(o_ref.dtype)


def _ssd(h, conv_w, conv_b, dt_bias, a_log, d_skip, norm_g, bsz, seq):
    n = bsz * seq
    nc = seq // SSD_CHUNK
    L = SSD_CHUNK

    def hspec(name):
        w, idx = _packed_block(name)
        return pl.BlockSpec((L, w), lambda b, c: (b * nc + c, idx))

    def full(shape):
        return pl.BlockSpec(shape, lambda b, c: (0,) * len(shape))

    pad = 128 - SSD_HEADS
    dtb = jnp.pad(dt_bias.astype(F32), (0, pad)).reshape(1, 128)
    a = jnp.pad(-jnp.exp(a_log.astype(F32)), (0, pad)).reshape(1, 128)
    dexp = jnp.repeat(d_skip.astype(F32), SSD_HEAD_DIM).reshape(1, SSD_WIDTH)
    return pl.pallas_call(
        _ssd_kernel,
        grid=(bsz, nc),
        in_specs=[hspec('ssd_z'), hspec('ssd_xbc'), hspec('ssd_dt'),
                  full((SSD_CONV, SSD_XBC)), full((1, SSD_XBC)), full((1, 128)), full((1, 128)),
                  full((1, SSD_WIDTH)), full((1, SSD_WIDTH))],
        out_specs=pl.BlockSpec((L, SSD_WIDTH), lambda b, c: (b * nc + c, 0)),
        out_shape=jax.ShapeDtypeStruct((n, SSD_WIDTH), BF16),
        scratch_shapes=[pltpu.VMEM((L + 8, SSD_XBC), F32),
                        pltpu.VMEM((SSD_HEADS, SSD_STATE, SSD_HEAD_DIM), F32),
                        pltpu.VMEM((L, SSD_WIDTH), F32)],
        compiler_params=_cparams(2),
    )(h, h, h, conv_w.astype(F32), conv_b.astype(F32).reshape(1, SSD_XBC), dtb, a, dexp,
      norm_g.astype(F32).reshape(1, SSD_WIDTH))


S5_T = 16
S5_CW = S5_T * S5_GROUP
S5_PAIRS = S5_GROUPS // 2


def _s5_tables(lam_re, lam_im, log_dt, b_re, b_im, c_re, c_im, d_skip):
    f32 = F32
    hp = lax.Precision.HIGHEST
    dt = jnp.exp(log_dt.astype(f32))[:, None]
    lr, li = lam_re.astype(f32), lam_im.astype(f32)
    mag = jnp.exp(lr * dt)
    ar, ai = mag * jnp.cos(li * dt), mag * jnp.sin(li * dt)
    den = lr * lr + li * li
    nr, ni = ar - 1.0, ai
    fr, fi = (nr * lr + ni * li) / den, (ni * lr - nr * li) / den
    bbr = fr[..., None] * b_re - fi[..., None] * b_im
    bbi = fr[..., None] * b_im + fi[..., None] * b_re
    taus = jnp.arange(S5_T + 1, dtype=f32)[:, None, None]
    pmag = jnp.exp(taus * (lr * dt)[None])
    pr, pi = pmag * jnp.cos(taus * (li * dt)[None]), pmag * jnp.sin(taus * (li * dt)[None])
    lbr = pr[..., None] * bbr[None] - pi[..., None] * bbi[None]
    lbi = pr[..., None] * bbi[None] + pi[..., None] * bbr[None]
    cr, ci = c_re.astype(f32), c_im.astype(f32)
    kmat = (jnp.einsum('ghp,tgpk->tghk', cr, lbr, precision=hp)
            - jnp.einsum('ghp,tgpk->tghk', ci, lbi, precision=hp))
    t_idx = jnp.arange(S5_T)
    diff = t_idx[None, :] - t_idx[:, None]
    kk = kmat[jnp.clip(diff, 0, S5_T)]
    kk = jnp.where((diff >= 0)[:, :, None, None, None], kk, 0.0)
    mt = jnp.transpose(kk, (2, 0, 4, 1, 3)).reshape(S5_GROUPS, S5_CW, S5_CW)
    wr = jnp.transpose(lbr[S5_T - 1 - t_idx], (1, 0, 3, 2)).reshape(S5_GROUPS, S5_CW, S5_STATE)
    wi = jnp.transpose(lbi[S5_T - 1 - t_idx], (1, 0, 3, 2)).reshape(S5_GROUPS, S5_CW, S5_STATE)
    pr1, pi1 = pr[1:], pi[1:]
    vr = cr[None] * pr1[:, :, None, :] - ci[None] * pi1[:, :, None, :]
    vi = cr[None] * pi1[:, :, None, :] + ci[None] * pr1[:, :, None, :]
    v_r = jnp.transpose(vr, (1, 3, 0, 2)).reshape(S5_GROUPS, S5_STATE, S5_CW)
    v_i = -jnp.transpose(vi, (1, 3, 0, 2)).reshape(S5_GROUPS, S5_STATE, S5_CW)
    z = jnp.zeros
    wr2, wi2 = wr.reshape(S5_PAIRS, 2, S5_CW, S5_STATE), wi.reshape(S5_PAIRS, 2, S5_CW, S5_STATE)
    zz = z((S5_PAIRS, S5_CW, S5_STATE), f32)
    wp = jnp.concatenate([jnp.concatenate([wr2[:, 0], zz, wi2[:, 0], zz], 2),
                          jnp.concatenate([zz, wr2[:, 1], zz, wi2[:, 1]], 2)], 1)
    vr2, vi2 = v_r.reshape(S5_PAIRS, 2, S5_STATE, S5_CW), v_i.reshape(S5_PAIRS, 2, S5_STATE, S5_CW)
    zv = z((S5_PAIRS, S5_STATE, S5_CW), f32)
    vp = jnp.concatenate([jnp.concatenate([vr2[:, 0], zv], 2), jnp.concatenate([zv, vr2[:, 1]], 2),
                          jnp.concatenate([vi2[:, 0], zv], 2), jnp.concatenate([zv, vi2[:, 1]], 2)], 1)
    lam_t = (pr[S5_T].reshape(1, S5_GROUPS * S5_STATE), pi[S5_T].reshape(1, S5_GROUPS * S5_STATE))
    dexp = jnp.tile(d_skip.astype(f32)[:, None, :], (1, S5_T, 1)).reshape(S5_GROUPS, 1, S5_CW)
    return mt.astype(BF16), wp.astype(BF16), vp.astype(BF16), lam_t[0], lam_t[1], dexp


def _s5_kernel(u_ref, mt_ref, w_ref, v_ref, ar_ref, ai_ref, d_ref, o_ref,
               zr_ref, zi_ref, sr_ref, si_ref, cr_ref, ci_ref, *, rows):
    @pl.when(pl.program_id(1) == 0)
    def _():
        cr_ref[...] = jnp.zeros(cr_ref.shape, F32)
        ci_ref[...] = jnp.zeros(ci_ref.shape, F32)

    for p in range(S5_PAIRS):
        ub = jnp.concatenate([u_ref[2 * p], u_ref[2 * p + 1]], axis=1).astype(BF16)
        zz = jnp.dot(ub, w_ref[p], preferred_element_type=F32)
        zr_ref[:, p * 128:(p + 1) * 128] = zz[:, :128]
        zi_ref[:, p * 128:(p + 1) * 128] = zz[:, 128:]

    ar, ai = ar_ref[...], ai_ref[...]

    def step(r, carry):
        sr, si = carry
        sr_ref[pl.ds(r, 1), :] = sr
        si_ref[pl.ds(r, 1), :] = si
        zr = zr_ref[pl.ds(r, 1), :]
        zi = zi_ref[pl.ds(r, 1), :]
        return ar * sr - ai * si + zr, ar * si + ai * sr + zi

    sr, si = lax.fori_loop(0, rows, step, (cr_ref[...], ci_ref[...]))
    cr_ref[...] = sr
    ci_ref[...] = si

    for p in range(S5_PAIRS):
        u2 = jnp.concatenate([u_ref[2 * p], u_ref[2 * p + 1]], axis=1)
        ub = u2.astype(BF16)
        y = jnp.concatenate([jnp.dot(ub[:, :S5_CW], mt_ref[2 * p], preferred_element_type=F32),
                             jnp.dot(ub[:, S5_CW:], mt_ref[2 * p + 1], preferred_element_type=F32)], axis=1)
        sin = jnp.concatenate([sr_ref[:, p * 128:(p + 1) * 128], si_ref[:, p * 128:(p + 1) * 128]], axis=1)
        y = y + jnp.dot(sin.astype(BF16), v_ref[p], preferred_element_type=F32)
        y = y + u2 * jnp.concatenate([d_ref[2 * p], d_ref[2 * p + 1]], axis=1)
        y = jax.nn.gelu(y)
        o_ref[2 * p] = y[:, :S5_CW]
        o_ref[2 * p + 1] = y[:, S5_CW:]


def _s5(u4, tables, bsz, nch, rows):
    mtp, wp, vp, lar, lai, dexp = tables
    rows = min(rows, nch)
    nt = nch // rows
    gs = S5_GROUPS * S5_STATE

    def full(shape):
        return pl.BlockSpec(shape, lambda b, i: (0,) * len(shape))

    kern = functools.partial(_s5_kernel, rows=rows)
    return pl.pallas_call(
        kern,
        grid=(bsz, nt),
        in_specs=[pl.BlockSpec((None, S5_GROUPS, rows, S5_CW), lambda b, i: (b, 0, i, 0)),
                  full(mtp.shape), full(wp.shape), full(vp.shape), full((1, gs)), full((1, gs)),
                  full(dexp.shape)],
        out_specs=pl.BlockSpec((None, S5_GROUPS, rows, S5_CW), lambda b, i: (b, 0, i, 0)),
        out_shape=jax.ShapeDtypeStruct((bsz, S5_GROUPS, nch, S5_CW), F32),
        scratch_shapes=[pltpu.VMEM((rows, gs), F32), pltpu.VMEM((rows, gs), F32),
                        pltpu.VMEM((rows, gs), F32), pltpu.VMEM((rows, gs), F32),
                        pltpu.VMEM((1, gs), F32), pltpu.VMEM((1, gs), F32)],
        compiler_params=_cparams(2),
    )(u4, mtp, wp, vp, lar, lai, dexp)


def _mem_kernel(q_ref, z_ref, mk_ref, mv_ref, o_ref):
    scale = HEAD_DIM ** -0.5
    mk = mk_ref[...].astype(BF16)
    mv = mv_ref[...].astype(BF16)
    outs = []
    for h in range(MEM_HEADS):
        sl = slice(h * HEAD_DIM, (h + 1) * HEAD_DIM)
        qh = (q_ref[:, sl] * scale).astype(BF16)
        lg = lax.dot_general(qh, mk[:, sl], NT_DIMS, preferred_element_type=F32)
        m = jnp.max(lg, axis=-1, keepdims=True)
        p = jnp.exp(lg - m)
        p = p * (1.0 / jnp.sum(p, axis=-1, keepdims=True))
        outs.append(jnp.dot(p.astype(BF16), mv[:, sl], preferred_element_type=F32))
    o_ref[...] = (jnp.concatenate(outs, axis=1) * _silu(z_ref[...])).astype(o_ref.dtype)


def _mem_attn(h, memkv, bsz, seq, t=512):
    n = bsz * seq
    t = min(t, seq)
    nt = seq // t
    qw, qidx = _packed_block('mem_q')
    zw, zidx = _packed_block('mem_z')
    return pl.pallas_call(
        _mem_kernel,
        grid=(bsz, nt),
        in_specs=[pl.BlockSpec((t, qw), lambda b, i: (b * nt + i, qidx)),
                  pl.BlockSpec((t, zw), lambda b, i: (b * nt + i, zidx)),
                  pl.BlockSpec((MEM_LEN, MEM_WIDTH), lambda b, i: (b, 0)),
                  pl.BlockSpec((MEM_LEN, MEM_WIDTH), lambda b, i: (b, 1))],
        out_specs=pl.BlockSpec((t, MEM_WIDTH), lambda b, i: (b * nt + i, 0)),
        out_shape=jax.ShapeDtypeStruct((n, MEM_WIDTH), BF16),
        compiler_params=_cparams(2),
    )(h, h, memkv, memkv)


def _merge_kernel(x_ref, s5_ref, s5z_ref, att_ref, ssd_ref, mem_ref,
                  wglu_ref, bglu_ref, wg_ref, bg_ref, ws5_ref, watt_ref, wssd_ref, wmem_ref,
                  wout_ref, lng_ref, lnb_ref, o_ref, xb_ref, ys5_ref, acc_ref):
    nidx = pl.program_id(1)

    @pl.when(nidx == 0)
    def _():
        xb_ref[...] = x_ref[...].astype(BF16)
        y = s5_ref[...]
        glu = y * _sigmoid(jnp.dot(y.astype(BF16), wglu_ref[...], preferred_element_type=F32) + bglu_ref[...])
        ys5_ref[...] = (glu * _silu(s5z_ref[...])).astype(BF16)
        acc_ref[...] = jnp.zeros(acc_ref.shape, F32)

    xb = xb_ref[...]
    branches = ((ys5_ref, ws5_ref), (att_ref, watt_ref), (ssd_ref, wssd_ref), (mem_ref, wmem_ref))
    merged = None
    for i, (y_ref, w_ref) in enumerate(branches):
        gate = _sigmoid(jnp.dot(xb, wg_ref[i], preferred_element_type=F32) + bg_ref[i])
        term = gate * jnp.dot(y_ref[...], w_ref[...], preferred_element_type=F32)
        merged = term if merged is None else merged + term
    acc_ref[...] += jnp.dot(merged.astype(BF16), wout_ref[...], preferred_element_type=F32)

    @pl.when(nidx == pl.num_programs(1) - 1)
    def _():
        r = DEEPNORM_ALPHA * x_ref[...] + acc_ref[...]
        mu = jnp.mean(r, axis=-1, keepdims=True)
        rc = r - mu
        var = jnp.mean(rc * rc, axis=-1, keepdims=True)
        o_ref[...] = rc * lax.rsqrt(var + LN_EPS) * lng_ref[...] + lnb_ref[...]


def _merge(x, h, ys5, yatt, yssd, ymem, wglu, bglu, wg4, bg4, ws5, watt, wssd, wmem, wout, lng, lnb,
           tm=512, tn=256):
    n = x.shape[0]
    tm = min(tm, n)
    zw, zidx = _packed_block('s5_z')
    row = lambda w: pl.BlockSpec((tm, w), lambda i, c: (i, 0))
    colw = lambda k: pl.BlockSpec((k, tn), lambda i, c: (0, c))
    const = lambda shape: pl.BlockSpec(shape, lambda i, c: (0,) * len(shape))
    return pl.pallas_call(
        _merge_kernel,
        grid=(n // tm, D_MODEL // tn),
        in_specs=[row(D_MODEL), row(S5_WIDTH), pl.BlockSpec((tm, zw), lambda i, c: (i, zidx)),
                  row(ATTN_WIDTH), row(SSD_WIDTH), row(MEM_WIDTH),
                  const((S5_WIDTH, S5_WIDTH)), const((1, S5_WIDTH)),
                  pl.BlockSpec((N_BRANCH, D_MODEL, tn), lambda i, c: (0, 0, c)),
                  pl.BlockSpec((N_BRANCH, 1, tn), lambda i, c: (0, 0, c)),
                  colw(S5_WIDTH), colw(ATTN_WIDTH), colw(SSD_WIDTH), colw(MEM_WIDTH),
                  pl.BlockSpec((tn, D_MODEL), lambda i, c: (c, 0)),
                  const((1, D_MODEL)), const((1, D_MODEL))],
        out_specs=pl.BlockSpec((tm, D_MODEL), lambda i, c: (i, 0)),
        out_shape=jax.ShapeDtypeStruct((n, D_MODEL), F32),
        scratch_shapes=[pltpu.VMEM((tm, D_MODEL), BF16), pltpu.VMEM((tm, S5_WIDTH), BF16),
                        pltpu.VMEM((tm, D_MODEL), F32)],
        compiler_params=_cparams(2),
    )(x, ys5, h, yatt, yssd, ymem, wglu, bglu, wg4, bg4, ws5, watt, wssd, wmem, wout, lng, lnb)


def _split_w_in(w):
    cols, off = {}, 0
    for name, width in SPLITS:
        cols[name] = w[:, off:off + width]
        off += width
    d = w.shape[0]
    pieces = []
    for name, width in PACKED:
        if name == 'idx_kw':
            pieces += [cols['idx_k'], cols['idx_w'], jnp.zeros((d, width - IDX_DIM - IDX_HEADS), w.dtype)]
        elif name == 'ssd_dt':
            pieces += [cols['ssd_dt'], jnp.zeros((d, width - SSD_HEADS), w.dtype)]
        elif name == 'pad':
            pieces.append(jnp.zeros((d, width), w.dtype))
        else:
            pieces.append(cols[name])
    packed = jnp.concatenate(pieces, axis=1).astype(BF16)
    gates = cols['gates'].reshape(d, N_BRANCH, D_MODEL).transpose(1, 0, 2).astype(BF16)
    return packed, gates


def _rope_tables(positions):
    half = ROPE_DIM // 2
    inv = ROPE_THETA ** (-jnp.arange(half, dtype=F32) * 2.0 / ROPE_DIM)
    ang = positions.astype(F32).reshape(-1)[:, None] * inv
    cos, sin = jnp.cos(ang), jnp.sin(ang)
    n = ang.shape[0]
    ones = jnp.ones((n, HEAD_DIM - ROPE_DIM), F32)
    zeros = jnp.zeros((n, HEAD_DIM - ROPE_DIM), F32)
    zh = jnp.zeros((n, half), F32)
    c64 = jnp.concatenate([cos, cos, ones], axis=1)
    sp64 = jnp.concatenate([zh, sin, zeros], axis=1)
    sm64 = jnp.concatenate([-sin, zh, zeros], axis=1)
    dup = lambda t: jnp.concatenate([t, t], axis=1)
    return dup(c64), dup(sp64), dup(sm64)


def kernel(x, mem, positions, w_in, b_gate, s5_lam_re, s5_lam_im, s5_log_dt, s5_b_re, s5_b_im, s5_c_re, s5_c_im, s5_d, s5_w_glu, s5_b_glu, ssd_conv_w, ssd_conv_b, ssd_dt_bias, ssd_a_log, ssd_d, ssd_norm_g, mem_w_kv, w_br_s5, w_br_attn, w_br_ssd, w_br_mem, w_out, ln_g, ln_b):
    bsz, seq, d = x.shape
    n = bsz * seq
    depth = w_in.shape[0]
    qb = min(256, seq)
    nq = seq // qb
    nch = seq // S5_T
    ctab, sptab, smtab = _rope_tables(positions)
    xf = x.reshape(n, d).astype(F32)
    memf = mem.reshape(bsz * MEM_LEN, d).astype(F32)
    s5w, s5idx = _packed_block('s5_u')

    for i in range(depth):
        w_packed, w_gates = _split_w_in(w_in[i])
        h = _matmul(xf, w_packed, 1024, 512)
        memkv = _matmul(memf, mem_w_kv[i].astype(BF16), 512, 512)

        u4 = (h[:, s5idx * s5w:(s5idx + 1) * s5w]
              .reshape(bsz, nch, S5_T, S5_GROUPS, S5_GROUP).transpose(0, 3, 1, 2, 4)
              .reshape(bsz, S5_GROUPS, nch, S5_CW))
        tables = _s5_tables(s5_lam_re[i], s5_lam_im[i], s5_log_dt[i], s5_b_re[i], s5_b_im[i],
                            s5_c_re[i], s5_c_im[i], s5_d[i])
        y4 = _s5(u4, tables, bsz, nch, 128)
        ys5 = (y4.reshape(bsz, S5_GROUPS, nch, S5_T, S5_GROUP).transpose(0, 2, 3, 1, 4)
               .reshape(n, S5_WIDTH))

        qp, iqp, kr, vb, ki, wi = _prep(h, ctab, sptab, smtab)
        wt = wi.reshape(bsz, seq, IDX_HEADS).transpose(0, 2, 1)
        vt4 = _value_layout(vb, bsz, nq, qb)
        yatt = _dsa(h, qp, iqp, wt, ki, kr, vt4, bsz, seq, qb)

        yssd = _ssd(h, ssd_conv_w[i], ssd_conv_b[i], ssd_dt_bias[i], ssd_a_log[i], ssd_d[i],
                    ssd_norm_g[i], bsz, seq)

        ymem = _mem_attn(h, memkv, bsz, seq)

        xf = _merge(xf, h, ys5, yatt, yssd, ymem,
                    s5_w_glu[i].astype(BF16), s5_b_glu[i].astype(F32).reshape(1, S5_WIDTH),
                    w_gates, b_gate[i].astype(F32).reshape(N_BRANCH, 1, D_MODEL),
                    w_br_s5[i].astype(BF16), w_br_attn[i].astype(BF16), w_br_ssd[i].astype(BF16),
                    w_br_mem[i].astype(BF16), w_out[i].astype(BF16),
                    ln_g[i].astype(F32).reshape(1, D_MODEL), ln_b[i].astype(F32).reshape(1, D_MODEL))
    return xf.reshape(bsz, seq, d).astype(x.dtype)
```

```python
import functools
import math

import jax
import jax.numpy as jnp
from jax import lax
from jax.experimental import pallas as pl
from jax.experimental.pallas import tpu as pltpu

F32 = jnp.float32
BF16 = jnp.bfloat16

D_MODEL = 2048
S5_WIDTH = 512
S5_GROUP = 16
S5_GROUPS = 32
S5_STATE = 64
HEAD_DIM = 64
ATTN_HEADS = 8
ATTN_KV_HEADS = 2
ATTN_WIDTH = 512
IDX_HEADS = 4
IDX_DIM = 64
TOPK_MAX = 256
SSD_WIDTH = 1024
SSD_HEAD_DIM = 64
SSD_HEADS = 16
SSD_GROUPS = 2
SSD_STATE = 128
SSD_CONV = 4
SSD_CHUNK = 128
SSD_XBC = 1536
MEM_LEN = 256
MEM_HEADS = 4
MEM_WIDTH = 256
N_BRANCH = 4
ROPE_THETA = 500000.0
ROPE_DIM = 16
DEPTH = 2
DEEPNORM_ALPHA = (2 * DEPTH) ** 0.25
LN_EPS = 1e-5
RMS_EPS = 1e-5

SPLITS = (
    ('s5_u', 512), ('s5_z', 512),
    ('att_q', 512), ('att_k', 128), ('att_v', 128), ('att_z', 512),
    ('idx_q', 256), ('idx_k', 64), ('idx_w', 4),
    ('ssd_z', 1024), ('ssd_xbc', 1536), ('ssd_dt', 16),
    ('mem_q', 256), ('mem_z', 256),
    ('gates', 8192),
)

PACKED = (
    ('ssd_z', 1024), ('s5_u', 512), ('ssd_xbc', 1536), ('s5_z', 512), ('att_q', 512), ('att_z', 512),
    ('idx_q', 256), ('mem_q', 256), ('mem_z', 256), ('att_k', 128), ('att_v', 128),
    ('idx_kw', 128), ('ssd_dt', 128), ('pad', 256),
)
PACKED_WIDTH = sum(w for _, w in PACKED)


def _packed_block(name):
    off = 0
    for n, w in PACKED:
        if n == name:
            assert off % w == 0
            return w, off // w
        off += w
    raise KeyError(name)


VMEM_LIMIT = 56 * 1024 * 1024


def _cparams(n_axes, vmem=VMEM_LIMIT):
    return pltpu.CompilerParams(dimension_semantics=("arbitrary",) * n_axes, vmem_limit_bytes=vmem)


def _silu(x):
    return x * (1.0 / (1.0 + jnp.exp(-x)))


def _sigmoid(x):
    return 1.0 / (1.0 + jnp.exp(-x))


def _mm_kernel(x_ref, w_ref, o_ref, xb_ref):
    @pl.when(pl.program_id(1) == 0)
    def _():
        xb_ref[...] = x_ref[...].astype(BF16)

    o_ref[...] = jnp.dot(xb_ref[...], w_ref[...], preferred_element_type=F32).astype(o_ref.dtype)


def _matmul(x, w, tm, tn, out_dtype=F32):
    m, k = x.shape
    n = w.shape[1]
    tm = min(tm, m)
    tn = min(tn, n)
    assert m % tm == 0 and n % tn == 0
    return pl.pallas_call(
        _mm_kernel,
        grid=(m // tm, n // tn),
        in_specs=[pl.BlockSpec((tm, k), lambda i, j: (i, 0)),
                  pl.BlockSpec((k, tn), lambda i, j: (0, j))],
        out_specs=pl.BlockSpec((tm, tn), lambda i, j: (i, j)),
        out_shape=jax.ShapeDtypeStruct((m, n), out_dtype),
        scratch_shapes=[pltpu.VMEM((tm, k), BF16)],
        compiler_params=_cparams(2),
    )(x, w)


def _rope128(x, c, sp, sm):
    return x * c + pltpu.roll(x, 8, 1) * sp + pltpu.roll(x, 120, 1) * sm


def _prep_kernel(q_ref, iq_ref, k_ref, v_ref, kw_ref, c_ref, sp_ref, sm_ref,
                 qo_ref, iqo_ref, ko_ref, vo_ref, kio_ref, wo_ref):
    c, sp, sm = c_ref[...], sp_ref[...], sm_ref[...]
    t = q_ref.shape[0]
    zeros64 = jnp.zeros((t, 64), F32)
    scale = HEAD_DIM ** -0.5 * math.log2(math.e)
    for pair in range(ATTN_HEADS // 2):
        r = _rope128(q_ref[:, pair * 128:(pair + 1) * 128], c, sp, sm) * scale
        for sub in range(2):
            h = pair * 2 + sub
            part = r[:, sub * 64:(sub + 1) * 64]
            if h // (ATTN_HEADS // ATTN_KV_HEADS) == 0:
                full = jnp.concatenate([part, zeros64], axis=1)
            else:
                full = jnp.concatenate([zeros64, part], axis=1)
            qo_ref[h] = full.astype(BF16)
    for pair in range(IDX_HEADS // 2):
        r = _rope128(iq_ref[:, pair * 128:(pair + 1) * 128], c, sp, sm)
        for sub in range(2):
            iqo_ref[pair * 2 + sub] = r[:, sub * 64:(sub + 1) * 64].astype(BF16)
    ko_ref[...] = _rope128(k_ref[...], c, sp, sm).astype(BF16)
    vo_ref[...] = v_ref[...].astype(BF16)
    kw = kw_ref[...]
    kio_ref[...] = _rope128(kw, c, sp, sm)[:, :64].astype(BF16)
    wo_ref[...] = kw[:, 64:64 + IDX_HEADS] * ((IDX_HEADS ** -0.5) * (IDX_DIM ** -0.5))


def _prep(h, ctab, sptab, smtab, t=512):
    n = h.shape[0]
    t = min(t, n)

    def hspec(name):
        w, idx = _packed_block(name)
        return pl.BlockSpec((t, w), lambda i: (i, idx))

    tspec = pl.BlockSpec((t, 128), lambda i: (i, 0))
    return pl.pallas_call(
        _prep_kernel,
        grid=(n // t,),
        in_specs=[hspec('att_q'), hspec('idx_q'), hspec('att_k'), hspec('att_v'), hspec('idx_kw'),
                  tspec, tspec, tspec],
        out_specs=[pl.BlockSpec((ATTN_HEADS, t, 128), lambda i: (0, i, 0)),
                   pl.BlockSpec((IDX_HEADS, t, 64), lambda i: (0, i, 0)),
                   pl.BlockSpec((t, 128), lambda i: (i, 0)),
                   pl.BlockSpec((t, 128), lambda i: (i, 0)),
                   pl.BlockSpec((t, 64), lambda i: (i, 0)),
                   pl.BlockSpec((t, IDX_HEADS), lambda i: (i, 0))],
        out_shape=[jax.ShapeDtypeStruct((ATTN_HEADS, n, 128), BF16),
                   jax.ShapeDtypeStruct((IDX_HEADS, n, 64), BF16),
                   jax.ShapeDtypeStruct((n, 128), BF16),
                   jax.ShapeDtypeStruct((n, 128), BF16),
                   jax.ShapeDtypeStruct((n, 64), BF16),
                   jax.ShapeDtypeStruct((n, IDX_HEADS), F32)],
        compiler_params=_cparams(1),
    )(h, h, h, h, h, ctab, sptab, smtab)


BISECT_STEPS = 15
V_ROWS = 80
NT_DIMS = (((1,), (1,)), ((), ()))


def _dsa_kernel(q_ref, iq_ref, wt_ref, az_ref, ki_ref, k_ref, vt_ref, o_ref,
                s_ref, acc_ref, l_ref, lg_ref, *, qb, topk):
    kb = qb
    j = pl.program_id(1)
    q0 = j * qb
    nk = j + 1
    neg = -jnp.inf
    qpos = q0 + lax.broadcasted_iota(jnp.int32, (1, qb), 1)
    row = lax.broadcasted_iota(jnp.int32, (kb, qb), 0)
    wt = wt_ref[...]

    def chunk(c):
        if isinstance(c, int):
            return pl.ds(c * kb, kb)
        return pl.ds(pl.multiple_of(c * kb, kb), kb)

    def p1(c, carry):
        mx, mn = carry
        kic = ki_ref[chunk(c), :]
        acc = jnp.zeros((kb, qb), F32)
        for h in range(IDX_HEADS):
            d = lax.dot_general(kic, iq_ref[h], NT_DIMS, preferred_element_type=F32)
            acc = acc + jnp.maximum(d, 0.0) * wt[h:h + 1, :]
        vis = (row + c * kb) <= qpos
        s_ref[chunk(c), :] = jnp.where(vis, acc, neg)
        mx = jnp.maximum(mx, jnp.max(jnp.where(vis, acc, neg), axis=0, keepdims=True))
        mn = jnp.minimum(mn, jnp.min(jnp.where(vis, acc, jnp.inf), axis=0, keepdims=True))
        return mx, mn

    mx, mn = lax.fori_loop(0, nk, p1, (jnp.full((1, qb), neg, F32), jnp.full((1, qb), jnp.inf, F32)))

    @pl.when(nk % 2 == 1)
    def _():
        s_ref[chunk(nk), :] = jnp.full((kb, qb), neg, F32)

    npair = (nk + 1) // 2

    sub = 64

    def sweep(fn, init):
        def body(i, a):
            base = pl.multiple_of(i * (2 * kb), 2 * kb)
            for r in range(0, 2 * kb, sub):
                a = fn(a, s_ref[pl.ds(base + r, sub), :])
            return a
        return lax.fori_loop(0, npair, body, init)

    def count(pred_fn):
        a = sweep(lambda a, s: a + jnp.where(pred_fn(s), 1.0, 0.0).reshape(sub // 8, 8, qb).sum(axis=0),
                  jnp.zeros((8, qb), F32))
        return jnp.sum(a, axis=0, keepdims=True)

    def max_below(hi):
        a = sweep(lambda a, s: jnp.maximum(a, jnp.where(s < hi, s, neg).reshape(sub // 8, 8, qb).max(axis=0)),
                  jnp.full((8, qb), neg, F32))
        return jnp.max(a, axis=0, keepdims=True)

    nvis = (qpos + 1).astype(F32)
    keff = jnp.minimum(nvis, float(topk))

    def bis_cond(carry):
        return jnp.logical_and(carry[0] > 0.0, carry[1] < BISECT_STEPS)

    def bis_body(carry):
        _, it, lo, hi, clo = carry
        top = jnp.minimum(hi, mx)
        mid = lo + 0.5 * (top - lo)
        cnt = count(lambda s: s >= mid)
        ok = cnt >= keff
        lo = jnp.where(ok, mid, lo)
        clo = jnp.where(ok, cnt, clo)
        hi = jnp.where(ok, hi, mid)
        return jnp.sum(jnp.where(clo == keff, 0.0, 1.0)), it + 1, lo, hi, clo

    _, _, lo, hi, clo = lax.while_loop(
        bis_cond, bis_body,
        (jnp.sum(jnp.where(nvis == keff, 0.0, 1.0)), jnp.int32(0), mn, jnp.full((1, qb), jnp.inf, F32), nvis))

    def fin_cond(carry):
        return carry[0] > 0.0

    def fin_body(carry):
        _, hi, thr, done = carry
        cand = max_below(hi)
        ok = count(lambda s: s >= cand) >= keff
        newly = jnp.logical_and(ok, done < 0.5)
        thr = jnp.where(newly, cand, thr)
        hi = jnp.where(jnp.logical_or(ok, done > 0.5), hi, cand)
        done = jnp.where(ok, 1.0, done)
        return jnp.sum(1.0 - done), hi, thr, done

    done0 = jnp.where(clo == keff, 1.0, 0.0)
    _, _, thr, _ = lax.while_loop(fin_cond, fin_body, (jnp.sum(1.0 - done0), hi, lo, done0))
    need = keff - count(lambda s: s > thr)

    l_ref[...] = jnp.zeros(l_ref.shape, F32)
    acc_ref[...] = jnp.zeros(acc_ref.shape, F32)
    tril = jnp.where(lax.broadcasted_iota(jnp.int32, (kb, kb), 1) <= lax.broadcasted_iota(jnp.int32, (kb, kb), 0),
                     1.0, 0.0).astype(BF16)

    def stage1(c, run, m_old):
        s = s_ref[chunk(c), :]
        eq = jnp.where(s == thr, 1.0, 0.0)
        rank = jnp.dot(tril, eq.astype(BF16), preferred_element_type=F32) + run
        tie_ok = jnp.where(s == thr, rank, jnp.inf) <= need
        bias = jnp.where(s > thr, 0.0, jnp.where(tie_ok, 0.0, neg))
        kc = k_ref[chunk(c), :]
        slot = c % 2
        cms = []
        for h in range(ATTN_HEADS):
            lg = lax.dot_general(kc, q_ref[h], NT_DIMS, preferred_element_type=F32) + bias
            lg_ref[slot, h] = lg
            cms.append(jnp.max(lg.reshape(kb // 8, 8, qb).max(axis=0), axis=0, keepdims=True))
        m_new = jnp.maximum(m_old, jnp.concatenate(cms, axis=0))
        m_safe = jnp.where(m_new == neg, 0.0, m_new)
        alpha = jnp.exp2(m_old - m_safe)
        return rank[kb - 1:kb, :], m_new, m_safe, alpha

    def stage2(c, m_safe, alpha):
        slot = c % 2
        lsum = []
        for h in range(ATTN_HEADS):
            g = h // (ATTN_HEADS // ATTN_KV_HEADS)
            p = jnp.exp2(lg_ref[slot, h] - m_safe[h:h + 1, :])
            pv = jnp.dot(vt_ref[c, g], p.astype(BF16), preferred_element_type=F32)
            lsum.append(pv[HEAD_DIM:HEAD_DIM + 1, :])
            acc_ref[h * 64:(h + 1) * 64, :] = alpha[h:h + 1, :] * acc_ref[h * 64:(h + 1) * 64, :] + pv[:HEAD_DIM, :]
        l_ref[...] = alpha * l_ref[...] + jnp.concatenate(lsum, axis=0)

    def p3(c, carry):
        run, m_old, m_safe, alpha = carry
        stage2(c - 1, m_safe, alpha)
        return stage1(c, run, m_old)

    carry = stage1(0, jnp.zeros((1, qb), F32), jnp.full((ATTN_HEADS, qb), neg, F32))
    _, _, m_safe, alpha = lax.fori_loop(1, nk, p3, carry)
    stage2(nk - 1, m_safe, alpha)

    for h in range(ATTN_HEADS):
        acc_ref[h * 64:(h + 1) * 64, :] = acc_ref[h * 64:(h + 1) * 64, :] * (1.0 / l_ref[h:h + 1, :])
    o_ref[...] = (acc_ref[...].T * _silu(az_ref[...])).astype(o_ref.dtype)


def _value_layout(vb, bsz, nq, qb):
    vt = vb.reshape(bsz, nq, qb, ATTN_KV_HEADS, HEAD_DIM).transpose(0, 1, 3, 4, 2)
    ones = jnp.ones((bsz, nq, ATTN_KV_HEADS, 1, qb), vb.dtype)
    zeros = jnp.zeros((bsz, nq, ATTN_KV_HEADS, V_ROWS - HEAD_DIM - 1, qb), vb.dtype)
    return jnp.concatenate([vt, ones, zeros], axis=3)


def _dsa(h, qp, iqp, wt, ki, kr, vt4, bsz, seq, qb):
    n = bsz * seq
    nq = seq // qb
    topk = min(TOPK_MAX, seq // 4)
    zw, zidx = _packed_block('att_z')
    kern = functools.partial(_dsa_kernel, qb=qb, topk=topk)
    return pl.pallas_call(
        kern,
        grid=(bsz, nq),
        in_specs=[pl.BlockSpec((ATTN_HEADS, qb, 128), lambda b, j: (0, b * nq + j, 0)),
                  pl.BlockSpec((IDX_HEADS, qb, 64), lambda b, j: (0, b * nq + j, 0)),
                  pl.BlockSpec((None, IDX_HEADS, qb), lambda b, j: (b, 0, j)),
                  pl.BlockSpec((qb, zw), lambda b, j: (b * nq + j, zidx)),
                  pl.BlockSpec((seq, 64), lambda b, j: (b, 0)),
                  pl.BlockSpec((seq, 128), lambda b, j: (b, 0)),
                  pl.BlockSpec((None, nq, ATTN_KV_HEADS, V_ROWS, qb), lambda b, j: (b, 0, 0, 0, 0))],
        out_specs=pl.BlockSpec((qb, ATTN_WIDTH), lambda b, j: (b * nq + j, 0)),
        out_shape=jax.ShapeDtypeStruct((n, ATTN_WIDTH), BF16),
        scratch_shapes=[pltpu.VMEM((seq + qb, qb), F32),
                        pltpu.VMEM((ATTN_WIDTH, qb), F32),
                        pltpu.VMEM((ATTN_HEADS, qb), F32),
                        pltpu.VMEM((2, ATTN_HEADS, qb, qb), F32)],
        compiler_params=_cparams(2),
    )(qp, iqp, wt, h, ki, kr, vt4)


def _ssd_kernel(z_ref, xbc_ref, dt_ref, cw_ref, cb_ref, dtb_ref, a_ref, d_ref, ng_ref, o_ref,
                xc_ref, st_ref, y_ref):
    c = pl.program_id(1)
    L = SSD_CHUNK

    @pl.when(c == 0)
    def _():
        xc_ref[0:8, :] = jnp.zeros((8, SSD_XBC), F32)
        st_ref[...] = jnp.zeros(st_ref.shape, F32)

    xc_ref[8:8 + L, :] = xbc_ref[...]
    conv = cb_ref[...] + cw_ref[3:4, :] * xc_ref[8:8 + L, :]
    for k in range(1, SSD_CONV):
        conv = conv + cw_ref[3 - k:4 - k, :] * xc_ref[8 - k:8 - k + L, :]
    xc_ref[0:8, :] = xc_ref[L:L + 8, :]
    xbc = _silu(conv)

    dtr = dt_ref[...] + dtb_ref[...]
    dt = jnp.maximum(dtr, 0.0) + jnp.log1p(jnp.exp(-jnp.abs(dtr)))
    da = dt * a_ref[...]
    ri = lax.broadcasted_iota(jnp.int32, (L, L), 0)
    ci = lax.broadcasted_iota(jnp.int32, (L, L), 1)
    causal = ci <= ri
    tri = jnp.where(causal, 1.0, 0.0)
    acs = jnp.dot(tri, da, preferred_element_type=F32, precision=lax.Precision.HIGHEST)
    acs_t = acs.T
    e_acs = jnp.exp(acs)
    e_last = jnp.exp(acs[L - 1:L, :])
    dec = jnp.exp(acs[L - 1:L, :] - acs)

    hpg = SSD_HEADS // SSD_GROUPS
    for g in range(SSD_GROUPS):
        bm = xbc[:, SSD_WIDTH + g * SSD_STATE:SSD_WIDTH + (g + 1) * SSD_STATE]
        cm = xbc[:, SSD_WIDTH + (SSD_GROUPS + g) * SSD_STATE:SSD_WIDTH + (SSD_GROUPS + g + 1) * SSD_STATE]
        bmb = bm.astype(BF16)
        cmb = cm.astype(BF16)
        cb = lax.dot_general(cmb, bmb, NT_DIMS, preferred_element_type=F32)
        bmt = bm.T.astype(BF16)
        for jj in range(hpg):
            hd = g * hpg + jj
            xj = xbc[:, hd * SSD_HEAD_DIM:(hd + 1) * SSD_HEAD_DIM]
            xdt = xj * dt[:, hd:hd + 1]
            seg = acs[:, hd:hd + 1] - acs_t[hd:hd + 1, :]
            lm = jnp.exp(jnp.where(causal, seg, -jnp.inf))
            y = jnp.dot((cb * lm).astype(BF16), xdt.astype(BF16), preferred_element_type=F32)
            prev = st_ref[hd]
            y = y + jnp.dot(cmb, prev.astype(BF16), preferred_element_type=F32) * e_acs[:, hd:hd + 1]
            st = jnp.dot(bmt, (xdt * dec[:, hd:hd + 1]).astype(BF16), preferred_element_type=F32)
            st_ref[hd] = prev * e_last[:, hd:hd + 1] + st
            y_ref[:, hd * SSD_HEAD_DIM:(hd + 1) * SSD_HEAD_DIM] = y + xj * d_ref[:, hd * SSD_HEAD_DIM:(hd + 1) * SSD_HEAD_DIM]

    y = y_ref[...] * _silu(z_ref[...])
    gw = SSD_WIDTH // SSD_GROUPS
    parts = []
    for g in range(SSD_GROUPS):
        yg = y[:, g * gw:(g + 1) * gw]
        parts.append(yg * lax.rsqrt(jnp.mean(yg * yg, axis=-1, keepdims=True) + RMS_EPS))
    o_ref[...] = (jnp.concatenate(parts, axis=1) * ng_ref[...]).astype(o_ref.dtype)


def _ssd(h, conv_w, conv_b, dt_bias, a_log, d_skip, norm_g, bsz, seq):
    n = bsz * seq
    nc = seq // SSD_CHUNK
    L = SSD_CHUNK

    def hspec(name):
        w, idx = _packed_block(name)
        return pl.BlockSpec((L, w), lambda b, c: (b * nc + c, idx))

    def full(shape):
        return pl.BlockSpec(shape, lambda b, c: (0,) * len(shape))

    pad = 128 - SSD_HEADS
    dtb = jnp.pad(dt_bias.astype(F32), (0, pad)).reshape(1, 128)
    a = jnp.pad(-jnp.exp(a_log.astype(F32)), (0, pad)).reshape(1, 128)
    dexp = jnp.repeat(d_skip.astype(F32), SSD_HEAD_DIM).reshape(1, SSD_WIDTH)
    return pl.pallas_call(
        _ssd_kernel,
        grid=(bsz, nc),
        in_specs=[hspec('ssd_z'), hspec('ssd_xbc'), hspec('ssd_dt'),
                  full((SSD_CONV, SSD_XBC)), full((1, SSD_XBC)), full((1, 128)), full((1, 128)),
                  full((1, SSD_WIDTH)), full((1, SSD_WIDTH))],
        out_specs=pl.BlockSpec((L, SSD_WIDTH), lambda b, c: (b * nc + c, 0)),
        out_shape=jax.ShapeDtypeStruct((n, SSD_WIDTH), BF16),
        scratch_shapes=[pltpu.VMEM((L + 8, SSD_XBC), F32),
                        pltpu.VMEM((SSD_HEADS, SSD_STATE, SSD_HEAD_DIM), F32),
                        pltpu.VMEM((L, SSD_WIDTH), F32)],
        compiler_params=_cparams(2),
    )(h, h, h, conv_w.astype(F32), conv_b.astype(F32).reshape(1, SSD_XBC), dtb, a, dexp,
      norm_g.astype(F32).reshape(1, SSD_WIDTH))


S5_T = 16
S5_SG_GROUPS = 128 // S5_GROUP
S5_SGS = S5_GROUPS // S5_SG_GROUPS
S5_LANES = S5_T * 128
S5_SG_STATE = S5_SG_GROUPS * S5_STATE


def _s5_tables(lam_re, lam_im, log_dt, b_re, b_im, c_re, c_im, d_skip):
    f32 = F32
    hp = lax.Precision.HIGHEST
    dt = jnp.exp(log_dt.astype(f32))[:, None]
    lr, li = lam_re.astype(f32), lam_im.astype(f32)
    mag = jnp.exp(lr * dt)
    ar, ai = mag * jnp.cos(li * dt), mag * jnp.sin(li * dt)
    den = lr * lr + li * li
    nr, ni = ar - 1.0, ai
    fr, fi = (nr * lr + ni * li) / den, (ni * lr - nr * li) / den
    bbr = fr[..., None] * b_re - fi[..., None] * b_im
    bbi = fr[..., None] * b_im + fi[..., None] * b_re
    taus = jnp.arange(S5_T + 1, dtype=f32)[:, None, None]
    pmag = jnp.exp(taus * (lr * dt)[None])
    pr, pi = pmag * jnp.cos(taus * (li * dt)[None]), pmag * jnp.sin(taus * (li * dt)[None])
    lbr = pr[..., None] * bbr[None] - pi[..., None] * bbi[None]
    lbi = pr[..., None] * bbi[None] + pi[..., None] * bbr[None]
    cr, ci = c_re.astype(f32), c_im.astype(f32)
    kmat = (jnp.einsum('ghp,tgpk->tghk', cr, lbr, precision=hp)
            - jnp.einsum('ghp,tgpk->tghk', ci, lbi, precision=hp))
    t_idx = jnp.arange(S5_T)
    diff = t_idx[None, :] - t_idx[:, None]
    kk = kmat[jnp.clip(diff, 0, S5_T)]
    kk = jnp.where((diff >= 0)[:, :, None, None, None], kk, 0.0)
    eye = jnp.eye(S5_SG_GROUPS, dtype=f32)
    sgs = (S5_SGS, S5_SG_GROUPS)
    kk6 = kk.reshape((S5_T, S5_T) + sgs + (S5_GROUP, S5_GROUP))
    mt = jnp.einsum('stagij,gk->asgjtki', kk6, eye).reshape(S5_SGS, S5_LANES, S5_LANES)
    lbr5 = lbr[S5_T - 1 - t_idx].reshape((S5_T,) + sgs + (S5_STATE, S5_GROUP))
    lbi5 = lbi[S5_T - 1 - t_idx].reshape((S5_T,) + sgs + (S5_STATE, S5_GROUP))
    wr = jnp.einsum('sagpj,gk->asgjkp', lbr5, eye).reshape(S5_SGS, S5_LANES, S5_SG_STATE)
    wi = jnp.einsum('sagpj,gk->asgjkp', lbi5, eye).reshape(S5_SGS, S5_LANES, S5_SG_STATE)
    wm = jnp.concatenate([wr, wi], axis=2)
    pr1, pi1 = pr[1:], pi[1:]
    vr = cr[None] * pr1[:, :, None, :] - ci[None] * pi1[:, :, None, :]
    vi = cr[None] * pi1[:, :, None, :] + ci[None] * pr1[:, :, None, :]
    vr5 = vr.reshape((S5_T,) + sgs + (S5_GROUP, S5_STATE))
    vi5 = vi.reshape((S5_T,) + sgs + (S5_GROUP, S5_STATE))
    v_r = jnp.einsum('tagip,gk->agptki', vr5, eye).reshape(S5_SGS, S5_SG_STATE, S5_LANES)
    v_i = -jnp.einsum('tagip,gk->agptki', vi5, eye).reshape(S5_SGS, S5_SG_STATE, S5_LANES)
    vm = jnp.concatenate([v_r, v_i], axis=1)
    lam_r = pr[S5_T].reshape(S5_SGS, 1, S5_SG_STATE)
    lam_i = pi[S5_T].reshape(S5_SGS, 1, S5_SG_STATE)
    dexp = jnp.tile(d_skip.astype(f32).reshape(S5_SGS, 1, 1, 128), (1, 1, S5_T, 1)).reshape(S5_SGS, 1, S5_LANES)
    return mt.astype(BF16), wm.astype(BF16), vm.astype(BF16), lam_r, lam_i, dexp


def _s5_kernel(u_ref, mt_ref, w_ref, v_ref, ar_ref, ai_ref, d_ref, o_ref,
               zr_ref, zi_ref, sr_ref, si_ref, cr_ref, ci_ref, *, rows):
    @pl.when(pl.program_id(2) == 0)
    def _():
        cr_ref[...] = jnp.zeros(cr_ref.shape, F32)
        ci_ref[...] = jnp.zeros(ci_ref.shape, F32)

    u = jnp.concatenate([u_ref[pl.ds(t, rows, stride=S5_T), :] for t in range(S5_T)], axis=1)
    ub = u.astype(BF16)

    zz = jnp.dot(ub, w_ref[...], preferred_element_type=F32)
    zr_ref[...] = zz[:, :S5_SG_STATE]
    zi_ref[...] = zz[:, S5_SG_STATE:]

    ar, ai = ar_ref[...], ai_ref[...]

    def step(r, carry):
        sr, si = carry
        sr_ref[pl.ds(r, 1), :] = sr
        si_ref[pl.ds(r, 1), :] = si
        zr = zr_ref[pl.ds(r, 1), :]
        zi = zi_ref[pl.ds(r, 1), :]
        return ar * sr - ai * si + zr, ar * si + ai * sr + zi

    sr, si = lax.fori_loop(0, rows, step, (cr_ref[...], ci_ref[...]))
    cr_ref[...] = sr
    ci_ref[...] = si

    sin = jnp.concatenate([sr_ref[...], si_ref[...]], axis=1).astype(BF16)
    yoff = jnp.dot(sin, v_ref[...], preferred_element_type=F32)
    tile = 256
    for ct in range(S5_LANES // tile):
        cols = slice(ct * tile, (ct + 1) * tile)
        kdim = (ct + 1) * tile
        y = jnp.dot(ub[:, :kdim], mt_ref[:kdim, cols], preferred_element_type=F32)
        y = jax.nn.gelu(y + yoff[:, cols] + u[:, cols] * d_ref[:, cols])
        for t in range(ct * tile // 128, (ct + 1) * tile // 128):
            o_ref[pl.ds(t, rows, stride=S5_T), :] = y[:, t * 128 - ct * tile:(t + 1) * 128 - ct * tile]


def _s5(h, tables, bsz, seq, rows):
    mt, wm, vm, lar, lai, dexp = tables
    n = bsz * seq
    nch = seq // S5_T
    rows = min(rows, nch)
    nt = nch // rows
    uw, uidx = _packed_block('s5_u')
    ublk = uidx * (uw // 128)

    def per_sg(shape):
        return pl.BlockSpec((None,) + shape, lambda a, b, i: (a, 0, 0))

    kern = functools.partial(_s5_kernel, rows=rows)
    return pl.pallas_call(
        kern,
        grid=(S5_SGS, bsz, nt),
        in_specs=[pl.BlockSpec((rows * S5_T, 128), lambda a, b, i: (b * nt + i, ublk + a)),
                  per_sg((S5_LANES, S5_LANES)), per_sg((S5_LANES, 2 * S5_SG_STATE)),
                  per_sg((2 * S5_SG_STATE, S5_LANES)), per_sg((1, S5_SG_STATE)), per_sg((1, S5_SG_STATE)),
                  per_sg((1, S5_LANES))],
        out_specs=pl.BlockSpec((rows * S5_T, 128), lambda a, b, i: (b * nt + i, a)),
        out_shape=jax.ShapeDtypeStruct((n, S5_WIDTH), F32),
        scratch_shapes=[pltpu.VMEM((rows, S5_SG_STATE), F32), pltpu.VMEM((rows, S5_SG_STATE), F32),
                        pltpu.VMEM((rows, S5_SG_STATE), F32), pltpu.VMEM((rows, S5_SG_STATE), F32),
                        pltpu.VMEM((1, S5_SG_STATE), F32), pltpu.VMEM((1, S5_SG_STATE), F32)],
        compiler_params=_cparams(3),
    )(h, mt, wm, vm, lar, lai, dexp)


def _mem_kernel(q_ref, z_ref, mk_ref, mv_ref, o_ref):
    scale = HEAD_DIM ** -0.5
    mk = mk_ref[...].astype(BF16)
    mv = mv_ref[...].astype(BF16)
    outs = []
    for h in range(MEM_HEADS):
        sl = slice(h * HEAD_DIM, (h + 1) * HEAD_DIM)
        qh = (q_ref[:, sl] * scale).astype(BF16)
        lg = lax.dot_general(qh, mk[:, sl], NT_DIMS, preferred_element_type=F32)
        m = jnp.max(lg, axis=-1, keepdims=True)
        p = jnp.exp(lg - m)
        p = p * (1.0 / jnp.sum(p, axis=-1, keepdims=True))
        outs.append(jnp.dot(p.astype(BF16), mv[:, sl], preferred_element_type=F32))
    o_ref[...] = (jnp.concatenate(outs, axis=1) * _silu(z_ref[...])).astype(o_ref.dtype)


def _mem_attn(h, memkv, bsz, seq, t=512):
    n = bsz * seq
    t = min(t, seq)
    nt = seq // t
    qw, qidx = _packed_block('mem_q')
    zw, zidx = _packed_block('mem_z')
    return pl.pallas_call(
        _mem_kernel,
        grid=(bsz, nt),
        in_specs=[pl.BlockSpec((t, qw), lambda b, i: (b * nt + i, qidx)),
                  pl.BlockSpec((t, zw), lambda b, i: (b * nt + i, zidx)),
                  pl.BlockSpec((MEM_LEN, MEM_WIDTH), lambda b, i: (b, 0)),
                  pl.BlockSpec((MEM_LEN, MEM_WIDTH), lambda b, i: (b, 1))],
        out_specs=pl.BlockSpec((t, MEM_WIDTH), lambda b, i: (b * nt + i, 0)),
        out_shape=jax.ShapeDtypeStruct((n, MEM_WIDTH), BF16),
        compiler_params=_cparams(2),
    )(h, h, memkv, memkv)


def _merge_kernel(x_ref, s5_ref, s5z_ref, att_ref, ssd_ref, mem_ref,
                  wglu_ref, bglu_ref, wg_ref, bg_ref, ws5_ref, watt_ref, wssd_ref, wmem_ref,
                  wout_ref, lng_ref, lnb_ref, o_ref, xb_ref, ys5_ref, acc_ref):
    nidx = pl.program_id(1)

    @pl.when(nidx == 0)
    def _():
        xb_ref[...] = x_ref[...].astype(BF16)
        y = s5_ref[...]
        glu = y * _sigmoid(jnp.dot(y.astype(BF16), wglu_ref[...], preferred_element_type=F32) + bglu_ref[...])
        ys5_ref[...] = (glu * _silu(s5z_ref[...])).astype(BF16)
        acc_ref[...] = jnp.zeros(acc_ref.shape, F32)

    xb = xb_ref[...]
    branches = ((ys5_ref, ws5_ref), (att_ref, watt_ref), (ssd_ref, wssd_ref), (mem_ref, wmem_ref))
    merged = None
    for i, (y_ref, w_ref) in enumerate(branches):
        gate = _sigmoid(jnp.dot(xb, wg_ref[i], preferred_element_type=F32) + bg_ref[i])
        term = gate * jnp.dot(y_ref[...], w_ref[...], preferred_element_type=F32)
        merged = term if merged is None else merged + term
    acc_ref[...] += jnp.dot(merged.astype(BF16), wout_ref[...], preferred_element_type=F32)

    @pl.when(nidx == pl.num_programs(1) - 1)
    def _():
        r = DEEPNORM_ALPHA * x_ref[...] + acc_ref[...]
        mu = jnp.mean(r, axis=-1, keepdims=True)
        rc = r - mu
        var = jnp.mean(rc * rc, axis=-1, keepdims=True)
        o_ref[...] = rc * lax.rsqrt(var + LN_EPS) * lng_ref[...] + lnb_ref[...]


def _merge(x, h, ys5, yatt, yssd, ymem, wglu, bglu, wg4, bg4, ws5, watt, wssd, wmem, wout, lng, lnb,
           tm=512, tn=256):
    n = x.shape[0]
    tm = min(tm, n)
    zw, zidx = _packed_block('s5_z')
    row = lambda w: pl.BlockSpec((tm, w), lambda i, c: (i, 0))
    colw = lambda k: pl.BlockSpec((k, tn), lambda i, c: (0, c))
    const = lambda shape: pl.BlockSpec(shape, lambda i, c: (0,) * len(shape))
    return pl.pallas_call(
        _merge_kernel,
        grid=(n // tm, D_MODEL // tn),
        in_specs=[row(D_MODEL), row(S5_WIDTH), pl.BlockSpec((tm, zw), lambda i, c: (i, zidx)),
                  row(ATTN_WIDTH), row(SSD_WIDTH), row(MEM_WIDTH),
                  const((S5_WIDTH, S5_WIDTH)), const((1, S5_WIDTH)),
                  pl.BlockSpec((N_BRANCH, D_MODEL, tn), lambda i, c: (0, 0, c)),
                  pl.BlockSpec((N_BRANCH, 1, tn), lambda i, c: (0, 0, c)),
                  colw(S5_WIDTH), colw(ATTN_WIDTH), colw(SSD_WIDTH), colw(MEM_WIDTH),
                  pl.BlockSpec((tn, D_MODEL), lambda i, c: (c, 0)),
                  const((1, D_MODEL)), const((1, D_MODEL))],
        out_specs=pl.BlockSpec((tm, D_MODEL), lambda i, c: (i, 0)),
        out_shape=jax.ShapeDtypeStruct((n, D_MODEL), F32),
        scratch_shapes=[pltpu.VMEM((tm, D_MODEL), BF16), pltpu.VMEM((tm, S5_WIDTH), BF16),
                        pltpu.VMEM((tm, D_MODEL), F32)],
        compiler_params=_cparams(2),
    )(x, ys5, h, yatt, yssd, ymem, wglu, bglu, wg4, bg4, ws5, watt, wssd, wmem, wout, lng, lnb)


def _split_w_in(w):
    cols, off = {}, 0
    for name, width in SPLITS:
        cols[name] = w[:, off:off + width]
        off += width
    d = w.shape[0]
    pieces = []
    for name, width in PACKED:
        if name == 'idx_kw':
            pieces += [cols['idx_k'], cols['idx_w'], jnp.zeros((d, width - IDX_DIM - IDX_HEADS), w.dtype)]
        elif name == 'ssd_dt':
            pieces += [cols['ssd_dt'], jnp.zeros((d, width - SSD_HEADS), w.dtype)]
        elif name == 'pad':
            pieces.append(jnp.zeros((d, width), w.dtype))
        else:
            pieces.append(cols[name])
    packed = jnp.concatenate(pieces, axis=1).astype(BF16)
    gates = cols['gates'].reshape(d, N_BRANCH, D_MODEL).transpose(1, 0, 2).astype(BF16)
    return packed, gates


def _rope_tables(positions):
    half = ROPE_DIM // 2
    inv = ROPE_THETA ** (-jnp.arange(half, dtype=F32) * 2.0 / ROPE_DIM)
    ang = positions.astype(F32).reshape(-1)[:, None] * inv
    cos, sin = jnp.cos(ang), jnp.sin(ang)
    n = ang.shape[0]
    ones = jnp.ones((n, HEAD_DIM - ROPE_DIM), F32)
    zeros = jnp.zeros((n, HEAD_DIM - ROPE_DIM), F32)
    zh = jnp.zeros((n, half), F32)
    c64 = jnp.concatenate([cos, cos, ones], axis=1)
    sp64 = jnp.concatenate([zh, sin, zeros], axis=1)
    sm64 = jnp.concatenate([-sin, zh, zeros], axis=1)
    dup = lambda t: jnp.concatenate([t, t], axis=1)
    return dup(c64), dup(sp64), dup(sm64)


def kernel(x, mem, positions, w_in, b_gate, s5_lam_re, s5_lam_im, s5_log_dt, s5_b_re, s5_b_im, s5_c_re, s5_c_im, s5_d, s5_w_glu, s5_b_glu, ssd_conv_w, ssd_conv_b, ssd_dt_bias, ssd_a_log, ssd_d, ssd_norm_g, mem_w_kv, w_br_s5, w_br_attn, w_br_ssd, w_br_mem, w_out, ln_g, ln_b):
    bsz, seq, d = x.shape
    n = bsz * seq
    depth = w_in.shape[0]
    qb = min(256, seq)
    nq = seq // qb
    ctab, sptab, smtab = _rope_tables(positions)
    xf = x.reshape(n, d).astype(F32)
    memf = mem.reshape(bsz * MEM_LEN, d).astype(F32)

    for i in range(depth):
        w_packed, w_gates = _split_w_in(w_in[i])
        h = _matmul(xf, w_packed, 1024, 512)
        memkv = _matmul(memf, mem_w_kv[i].astype(BF16), 512, 512)

        tables = _s5_tables(s5_lam_re[i], s5_lam_im[i], s5_log_dt[i], s5_b_re[i], s5_b_im[i],
                            s5_c_re[i], s5_c_im[i], s5_d[i])
        ys5 = _s5(h, tables, bsz, seq, 256)

        qp, iqp, kr, vb, ki, wi = _prep(h, ctab, sptab, smtab)
        wt = wi.reshape(bsz, seq, IDX_HEADS).transpose(0, 2, 1)
        vt4 = _value_layout(vb, bsz, nq, qb)
        yatt = _dsa(h, qp, iqp, wt, ki, kr, vt4, bsz, seq, qb)

        yssd = _ssd(h, ssd_conv_w[i], ssd_conv_b[i], ssd_dt_bias[i], ssd_a_log[i], ssd_d[i],
                    ssd_norm_g[i], bsz, seq)

        ymem = _mem_attn(h, memkv, bsz, seq)

        xf = _merge(xf, h, ys5, yatt, yssd, ymem,
                    s5_w_glu[i].astype(BF16), s5_b_glu[i].astype(F32).reshape(1, S5_WIDTH),
                    w_gates, b_gate[i].astype(F32).reshape(N_BRANCH, 1, D_MODEL),
                    w_br_s5[i].astype(BF16), w_br_attn[i].astype(BF16), w_br_ssd[i].astype(BF16),
                    w_br_mem[i].astype(BF16), w_out[i].astype(BF16),
                    ln_g[i].astype(F32).reshape(1, D_MODEL), ln_b[i].astype(F32).reshape(1, D_MODEL))
    return xf.reshape(bsz, seq, d).astype(x.dtype)
```

```python
import functools
import math

import jax
import jax.numpy as jnp
from jax import lax
from jax.experimental import pallas as pl
from jax.experimental.pallas import tpu as pltpu

F32 = jnp.float32
BF16 = jnp.bfloat16

D_MODEL = 2048
S5_WIDTH = 512
S5_GROUP = 16
S5_GROUPS = 32
S5_STATE = 64
HEAD_DIM = 64
ATTN_HEADS = 8
ATTN_KV_HEADS = 2
ATTN_WIDTH = 512
IDX_HEADS = 4
IDX_DIM = 64
TOPK_MAX = 256
SSD_WIDTH = 1024
SSD_HEAD_DIM = 64
SSD_HEADS = 16
SSD_GROUPS = 2
SSD_STATE = 128
SSD_CONV = 4
SSD_CHUNK = 128
SSD_XBC = 1536
MEM_LEN = 256
MEM_HEADS = 4
MEM_WIDTH = 256
N_BRANCH = 4
ROPE_THETA = 500000.0
ROPE_DIM = 16
DEPTH = 2
DEEPNORM_ALPHA = (2 * DEPTH) ** 0.25
LN_EPS = 1e-5
RMS_EPS = 1e-5

SPLITS = (
    ('s5_u', 512), ('s5_z', 512),
    ('att_q', 512), ('att_k', 128), ('att_v', 128), ('att_z', 512),
    ('idx_q', 256), ('idx_k', 64), ('idx_w', 4),
    ('ssd_z', 1024), ('ssd_xbc', 1536), ('ssd_dt', 16),
    ('mem_q', 256), ('mem_z', 256),
    ('gates', 8192),
)

PACKED = (
    ('ssd_z', 1024), ('s5_u', 512), ('ssd_xbc', 1536), ('s5_z', 512), ('att_q', 512), ('att_z', 512),
    ('idx_q', 256), ('mem_q', 256), ('mem_z', 256), ('att_k', 128), ('att_v', 128),
    ('idx_kw', 128), ('ssd_dt', 128), ('pad', 256),
)
PACKED_WIDTH = sum(w for _, w in PACKED)


def _packed_block(name):
    off = 0
    for n, w in PACKED:
        if n == name:
            assert off % w == 0
            return w, off // w
        off += w
    raise KeyError(name)


VMEM_LIMIT = 56 * 1024 * 1024


def _cparams(n_axes, vmem=VMEM_LIMIT):
    return pltpu.CompilerParams(dimension_semantics=("arbitrary",) * n_axes, vmem_limit_bytes=vmem)


def _silu(x):
    return x * (1.0 / (1.0 + jnp.exp(-x)))


def _sigmoid(x):
    return 1.0 / (1.0 + jnp.exp(-x))


def _mm_kernel(x_ref, w_ref, o_ref, xb_ref):
    @pl.when(pl.program_id(1) == 0)
    def _():
        xb_ref[...] = x_ref[...].astype(BF16)

    o_ref[...] = jnp.dot(xb_ref[...], w_ref[...], preferred_element_type=F32).astype(o_ref.dtype)


def _matmul(x, w, tm, tn, out_dtype=F32):
    m, k = x.shape
    n = w.shape[1]
    tm = min(tm, m)
    tn = min(tn, n)
    assert m % tm == 0 and n % tn == 0
    return pl.pallas_call(
        _mm_kernel,
        grid=(m // tm, n // tn),
        in_specs=[pl.BlockSpec((tm, k), lambda i, j: (i, 0)),
                  pl.BlockSpec((k, tn), lambda i, j: (0, j))],
        out_specs=pl.BlockSpec((tm, tn), lambda i, j: (i, j)),
        out_shape=jax.ShapeDtypeStruct((m, n), out_dtype),
        scratch_shapes=[pltpu.VMEM((tm, k), BF16)],
        compiler_params=_cparams(2),
    )(x, w)


def _rope128(x, c, sp, sm):
    return x * c + pltpu.roll(x, 8, 1) * sp + pltpu.roll(x, 120, 1) * sm


def _prep_kernel(q_ref, iq_ref, k_ref, v_ref, kw_ref, c_ref, sp_ref, sm_ref,
                 qo_ref, iqo_ref, ko_ref, vo_ref, kio_ref, wo_ref):
    c, sp, sm = c_ref[...], sp_ref[...], sm_ref[...]
    t = q_ref.shape[0]
    zeros64 = jnp.zeros((t, 64), F32)
    scale = HEAD_DIM ** -0.5 * math.log2(math.e)
    for pair in range(ATTN_HEADS // 2):
        r = _rope128(q_ref[:, pair * 128:(pair + 1) * 128], c, sp, sm) * scale
        for sub in range(2):
            h = pair * 2 + sub
            part = r[:, sub * 64:(sub + 1) * 64]
            if h // (ATTN_HEADS // ATTN_KV_HEADS) == 0:
                full = jnp.concatenate([part, zeros64], axis=1)
            else:
                full = jnp.concatenate([zeros64, part], axis=1)
            qo_ref[h] = full.astype(BF16)
    for pair in range(IDX_HEADS // 2):
        r = _rope128(iq_ref[:, pair * 128:(pair + 1) * 128], c, sp, sm)
        for sub in range(2):
            iqo_ref[pair * 2 + sub] = r[:, sub * 64:(sub + 1) * 64].astype(BF16)
    ko_ref[...] = _rope128(k_ref[...], c, sp, sm).astype(BF16)
    vt = v_ref[...].T
    nchunk, _, _, qb = vo_ref.shape
    tail = jnp.where(lax.broadcasted_iota(jnp.int32, (V_ROWS - HEAD_DIM, qb), 0) == 0, 1.0, 0.0).astype(BF16)
    for ci in range(nchunk):
        for g in range(ATTN_KV_HEADS):
            vo_ref[ci, g, 0:HEAD_DIM, :] = vt[g * HEAD_DIM:(g + 1) * HEAD_DIM, ci * qb:(ci + 1) * qb].astype(BF16)
            vo_ref[ci, g, HEAD_DIM:V_ROWS, :] = tail
    kw = kw_ref[...]
    kio_ref[...] = _rope128(kw, c, sp, sm)[:, :64].astype(BF16)
    wo_ref[...] = kw.T[64:64 + IDX_HEADS, :] * ((IDX_HEADS ** -0.5) * (IDX_DIM ** -0.5))


def _prep(h, ctab, sptab, smtab, bsz, seq, qb, t=512):
    n = bsz * seq
    t = min(t, seq)
    nt = seq // t
    assert t % qb == 0

    def hspec(name):
        w, idx = _packed_block(name)
        return pl.BlockSpec((t, w), lambda b, i: (b * nt + i, idx))

    def rows(w):
        return pl.BlockSpec((t, w), lambda b, i: (b * nt + i, 0))

    return pl.pallas_call(
        _prep_kernel,
        grid=(bsz, nt),
        in_specs=[hspec('att_q'), hspec('idx_q'), hspec('att_k'), hspec('att_v'), hspec('idx_kw'),
                  rows(128), rows(128), rows(128)],
        out_specs=[pl.BlockSpec((ATTN_HEADS, t, 128), lambda b, i: (0, b * nt + i, 0)),
                   pl.BlockSpec((IDX_HEADS, t, 64), lambda b, i: (0, b * nt + i, 0)),
                   rows(128),
                   pl.BlockSpec((None, t // qb, ATTN_KV_HEADS, V_ROWS, qb), lambda b, i: (b, i, 0, 0, 0)),
                   rows(64),
                   pl.BlockSpec((None, IDX_HEADS, t), lambda b, i: (b, 0, i))],
        out_shape=[jax.ShapeDtypeStruct((ATTN_HEADS, n, 128), BF16),
                   jax.ShapeDtypeStruct((IDX_HEADS, n, 64), BF16),
                   jax.ShapeDtypeStruct((n, 128), BF16),
                   jax.ShapeDtypeStruct((bsz, seq // qb, ATTN_KV_HEADS, V_ROWS, qb), BF16),
                   jax.ShapeDtypeStruct((n, 64), BF16),
                   jax.ShapeDtypeStruct((bsz, IDX_HEADS, seq), F32)],
        compiler_params=_cparams(2),
    )(h, h, h, h, h, ctab, sptab, smtab)


BISECT_STEPS = 15
V_ROWS = 80
NT_DIMS = (((1,), (1,)), ((), ()))


def _dsa_kernel(q_ref, iq_ref, wt_ref, az_ref, ki_ref, k_ref, vt_ref, o_ref,
                s_ref, acc_ref, l_ref, lg_ref, *, qb, topk):
    kb = qb
    j = pl.program_id(1)
    q0 = j * qb
    nk = j + 1
    neg = -jnp.inf
    qpos = q0 + lax.broadcasted_iota(jnp.int32, (1, qb), 1)
    row = lax.broadcasted_iota(jnp.int32, (kb, qb), 0)
    wt = wt_ref[...]

    def chunk(c):
        if isinstance(c, int):
            return pl.ds(c * kb, kb)
        return pl.ds(pl.multiple_of(c * kb, kb), kb)

    def p1(c, carry):
        mx, mn = carry
        kic = ki_ref[chunk(c), :]
        acc = jnp.zeros((kb, qb), F32)
        for h in range(IDX_HEADS):
            d = lax.dot_general(kic, iq_ref[h], NT_DIMS, preferred_element_type=F32)
            acc = acc + jnp.maximum(d, 0.0) * wt[h:h + 1, :]
        vis = (row + c * kb) <= qpos
        s_ref[chunk(c), :] = jnp.where(vis, acc, neg)
        mx = jnp.maximum(mx, jnp.max(jnp.where(vis, acc, neg), axis=0, keepdims=True))
        mn = jnp.minimum(mn, jnp.min(jnp.where(vis, acc, jnp.inf), axis=0, keepdims=True))
        return mx, mn

    mx, mn = lax.fori_loop(0, nk, p1, (jnp.full((1, qb), neg, F32), jnp.full((1, qb), jnp.inf, F32)))

    @pl.when(nk % 2 == 1)
    def _():
        s_ref[chunk(nk), :] = jnp.full((kb, qb), neg, F32)

    npair = (nk + 1) // 2

    sub = 64

    def sweep(fn, init):
        def body(i, a):
            base = pl.multiple_of(i * (2 * kb), 2 * kb)
            for r in range(0, 2 * kb, sub):
                a = fn(a, s_ref[pl.ds(base + r, sub), :])
            return a
        return lax.fori_loop(0, npair, body, init)

    def count(pred_fn):
        a = sweep(lambda a, s: a + jnp.where(pred_fn(s), 1.0, 0.0).reshape(sub // 8, 8, qb).sum(axis=0),
                  jnp.zeros((8, qb), F32))
        return jnp.sum(a, axis=0, keepdims=True)

    def max_below(hi):
        a = sweep(lambda a, s: jnp.maximum(a, jnp.where(s < hi, s, neg).reshape(sub // 8, 8, qb).max(axis=0)),
                  jnp.full((8, qb), neg, F32))
        return jnp.max(a, axis=0, keepdims=True)

    nvis = (qpos + 1).astype(F32)
    keff = jnp.minimum(nvis, float(topk))

    def bis_cond(carry):
        return jnp.logical_and(carry[0] > 0.0, carry[1] < BISECT_STEPS)

    def bis_body(carry):
        _, it, lo, hi, clo = carry
        top = jnp.minimum(hi, mx)
        mid = lo + 0.5 * (top - lo)
        cnt = count(lambda s: s >= mid)
        ok = cnt >= keff
        lo = jnp.where(ok, mid, lo)
        clo = jnp.where(ok, cnt, clo)
        hi = jnp.where(ok, hi, mid)
        return jnp.sum(jnp.where(clo == keff, 0.0, 1.0)), it + 1, lo, hi, clo

    _, _, lo, hi, clo = lax.while_loop(
        bis_cond, bis_body,
        (jnp.sum(jnp.where(nvis == keff, 0.0, 1.0)), jnp.int32(0), mn, jnp.full((1, qb), jnp.inf, F32), nvis))

    def fin_cond(carry):
        return carry[0] > 0.0

    def fin_body(carry):
        _, hi, thr, done = carry
        cand = max_below(hi)
        ok = count(lambda s: s >= cand) >= keff
        newly = jnp.logical_and(ok, done < 0.5)
        thr = jnp.where(newly, cand, thr)
        hi = jnp.where(jnp.logical_or(ok, done > 0.5), hi, cand)
        done = jnp.where(ok, 1.0, done)
        return jnp.sum(1.0 - done), hi, thr, done

    done0 = jnp.where(clo == keff, 1.0, 0.0)
    _, _, thr, _ = lax.while_loop(fin_cond, fin_body, (jnp.sum(1.0 - done0), hi, lo, done0))
    need = keff - count(lambda s: s > thr)

    l_ref[...] = jnp.zeros(l_ref.shape, F32)
    acc_ref[...] = jnp.zeros(acc_ref.shape, F32)
    tril = jnp.where(lax.broadcasted_iota(jnp.int32, (kb, kb), 1) <= lax.broadcasted_iota(jnp.int32, (kb, kb), 0),
                     1.0, 0.0).astype(BF16)

    def stage1(c, run, m_old):
        s = s_ref[chunk(c), :]
        eq = jnp.where(s == thr, 1.0, 0.0)
        rank = jnp.dot(tril, eq.astype(BF16), preferred_element_type=F32) + run
        tie_ok = jnp.where(s == thr, rank, jnp.inf) <= need
        bias = jnp.where(s > thr, 0.0, jnp.where(tie_ok, 0.0, neg))
        kc = k_ref[chunk(c), :]
        slot = c % 2
        cms = []
        for h in range(ATTN_HEADS):
            lg = lax.dot_general(kc, q_ref[h], NT_DIMS, preferred_element_type=F32) + bias
            lg_ref[slot, h] = lg
            cms.append(jnp.max(lg.reshape(kb // 8, 8, qb).max(axis=0), axis=0, keepdims=True))
        m_new = jnp.maximum(m_old, jnp.concatenate(cms, axis=0))
        m_safe = jnp.where(m_new == neg, 0.0, m_new)
        alpha = jnp.exp2(m_old - m_safe)
        return rank[kb - 1:kb, :], m_new, m_safe, alpha

    def stage2(c, m_safe, alpha):
        slot = c % 2
        lsum = []
        for h in range(ATTN_HEADS):
            g = h // (ATTN_HEADS // ATTN_KV_HEADS)
            p = jnp.exp2(lg_ref[slot, h] - m_safe[h:h + 1, :])
            pv = jnp.dot(vt_ref[c, g], p.astype(BF16), preferred_element_type=F32)
            lsum.append(pv[HEAD_DIM:HEAD_DIM + 1, :])
            acc_ref[h * 64:(h + 1) * 64, :] = alpha[h:h + 1, :] * acc_ref[h * 64:(h + 1) * 64, :] + pv[:HEAD_DIM, :]
        l_ref[...] = alpha * l_ref[...] + jnp.concatenate(lsum, axis=0)

    def p3(c, carry):
        run, m_old, m_safe, alpha = carry
        stage2(c - 1, m_safe, alpha)
        return stage1(c, run, m_old)

    carry = stage1(0, jnp.zeros((1, qb), F32), jnp.full((ATTN_HEADS, qb), neg, F32))
    _, _, m_safe, alpha = lax.fori_loop(1, nk, p3, carry)
    stage2(nk - 1, m_safe, alpha)

    for h in range(ATTN_HEADS):
        acc_ref[h * 64:(h + 1) * 64, :] = acc_ref[h * 64:(h + 1) * 64, :] * (1.0 / l_ref[h:h + 1, :])
    o_ref[...] = (acc_ref[...].T * _silu(az_ref[...])).astype(o_ref.dtype)


def _dsa(h, qp, iqp, wt, ki, kr, vt4, bsz, seq, qb):
    n = bsz * seq
    nq = seq // qb
    topk = min(TOPK_MAX, seq // 4)
    zw, zidx = _packed_block('att_z')
    kern = functools.partial(_dsa_kernel, qb=qb, topk=topk)
    return pl.pallas_call(
        kern,
        grid=(bsz, nq),
        in_specs=[pl.BlockSpec((ATTN_HEADS, qb, 128), lambda b, j: (0, b * nq + j, 0)),
                  pl.BlockSpec((IDX_HEADS, qb, 64), lambda b, j: (0, b * nq + j, 0)),
                  pl.BlockSpec((None, IDX_HEADS, qb), lambda b, j: (b, 0, j)),
                  pl.BlockSpec((qb, zw), lambda b, j: (b * nq + j, zidx)),
                  pl.BlockSpec((seq, 64), lambda b, j: (b, 0)),
                  pl.BlockSpec((seq, 128), lambda b, j: (b, 0)),
                  pl.BlockSpec((None, nq, ATTN_KV_HEADS, V_ROWS, qb), lambda b, j: (b, 0, 0, 0, 0))],
        out_specs=pl.BlockSpec((qb, ATTN_WIDTH), lambda b, j: (b * nq + j, 0)),
        out_shape=jax.ShapeDtypeStruct((n, ATTN_WIDTH), BF16),
        scratch_shapes=[pltpu.VMEM((seq + qb, qb), F32),
                        pltpu.VMEM((ATTN_WIDTH, qb), F32),
                        pltpu.VMEM((ATTN_HEADS, qb), F32),
                        pltpu.VMEM((2, ATTN_HEADS, qb, qb), F32)],
        compiler_params=_cparams(2),
    )(qp, iqp, wt, h, ki, kr, vt4)


def _ssd_kernel(z_ref, xbc_ref, dt_ref, cw_ref, cb_ref, dtb_ref, a_ref, d_ref, ng_ref, o_ref,
                xc_ref, st_ref, y_ref):
    c = pl.program_id(1)
    L = SSD_CHUNK

    @pl.when(c == 0)
    def _():
        xc_ref[0:8, :] = jnp.zeros((8, SSD_XBC), F32)
        st_ref[...] = jnp.zeros(st_ref.shape, F32)

    xc_ref[8:8 + L, :] = xbc_ref[...]
    conv = cb_ref[...] + cw_ref[3:4, :] * xc_ref[8:8 + L, :]
    for k in range(1, SSD_CONV):
        conv = conv + cw_ref[3 - k:4 - k, :] * xc_ref[8 - k:8 - k + L, :]
    xc_ref[0:8, :] = xc_ref[L:L + 8, :]
    xbc = _silu(conv)

    dtr = dt_ref[...] + dtb_ref[...]
    dt = jnp.maximum(dtr, 0.0) + jnp.log1p(jnp.exp(-jnp.abs(dtr)))
    da = dt * a_ref[...]
    ri = lax.broadcasted_iota(jnp.int32, (L, L), 0)
    ci = lax.broadcasted_iota(jnp.int32, (L, L), 1)
    causal = ci <= ri
    tri = jnp.where(causal, 1.0, 0.0)
    acs = jnp.dot(tri, da, preferred_element_type=F32, precision=lax.Precision.HIGHEST)
    acs_t = acs.T
    e_acs = jnp.exp(acs)
    e_last = jnp.exp(acs[L - 1:L, :])
    dec = jnp.exp(acs[L - 1:L, :] - acs)

    hpg = SSD_HEADS // SSD_GROUPS
    for g in range(SSD_GROUPS):
        bm = xbc[:, SSD_WIDTH + g * SSD_STATE:SSD_WIDTH + (g + 1) * SSD_STATE]
        cm = xbc[:, SSD_WIDTH + (SSD_GROUPS + g) * SSD_STATE:SSD_WIDTH + (SSD_GROUPS + g + 1) * SSD_STATE]
        bmb = bm.astype(BF16)
        cmb = cm.astype(BF16)
        cb = lax.dot_general(cmb, bmb, NT_DIMS, preferred_element_type=F32)
        bmt = bm.T.astype(BF16)
        for jj in range(hpg):
            hd = g * hpg + jj
            xj = xbc[:, hd * SSD_HEAD_DIM:(hd + 1) * SSD_HEAD_DIM]
            xdt = xj * dt[:, hd:hd + 1]
            seg = acs[:, hd:hd + 1] - acs_t[hd:hd + 1, :]
            lm = jnp.exp(jnp.where(causal, seg, -jnp.inf))
            y = jnp.dot((cb * lm).astype(BF16), xdt.astype(BF16), preferred_element_type=F32)
            prev = st_ref[hd]
            y = y + jnp.dot(cmb, prev.astype(BF16), preferred_element_type=F32) * e_acs[:, hd:hd + 1]
            st = jnp.dot(bmt, (xdt * dec[:, hd:hd + 1]).astype(BF16), preferred_element_type=F32)
            st_ref[hd] = prev * e_last[:, hd:hd + 1] + st
            y_ref[:, hd * SSD_HEAD_DIM:(hd + 1) * SSD_HEAD_DIM] = y + xj * d_ref[:, hd * SSD_HEAD_DIM:(hd + 1) * SSD_HEAD_DIM]

    y = y_ref[...] * _silu(z_ref[...])
    gw = SSD_WIDTH // SSD_GROUPS
    parts = []
    for g in range(SSD_GROUPS):
        yg = y[:, g * gw:(g + 1) * gw]
        parts.append(yg * lax.rsqrt(jnp.mean(yg * yg, axis=-1, keepdims=True) + RMS_EPS))
    o_ref[...] = (jnp.concatenate(parts, axis=1) * ng_ref[...]).astype(o_ref.dtype)


def _ssd(h, conv_w, conv_b, dt_bias, a_log, d_skip, norm_g, bsz, seq):
    n = bsz * seq
    nc = seq // SSD_CHUNK
    L = SSD_CHUNK

    def hspec(name):
        w, idx = _packed_block(name)
        return pl.BlockSpec((L, w), lambda b, c: (b * nc + c, idx))

    def full(shape):
        return pl.BlockSpec(shape, lambda b, c: (0,) * len(shape))

    pad = 128 - SSD_HEADS
    dtb = jnp.pad(dt_bias.astype(F32), (0, pad)).reshape(1, 128)
    a = jnp.pad(-jnp.exp(a_log.astype(F32)), (0, pad)).reshape(1, 128)
    dexp = jnp.repeat(d_skip.astype(F32), SSD_HEAD_DIM).reshape(1, SSD_WIDTH)
    return pl.pallas_call(
        _ssd_kernel,
        grid=(bsz, nc),
        in_specs=[hspec('ssd_z'), hspec('ssd_xbc'), hspec('ssd_dt'),
                  full((SSD_CONV, SSD_XBC)), full((1, SSD_XBC)), full((1, 128)), full((1, 128)),
                  full((1, SSD_WIDTH)), full((1, SSD_WIDTH))],
        out_specs=pl.BlockSpec((L, SSD_WIDTH), lambda b, c: (b * nc + c, 0)),
        out_shape=jax.ShapeDtypeStruct((n, SSD_WIDTH), BF16),
        scratch_shapes=[pltpu.VMEM((L + 8, SSD_XBC), F32),
                        pltpu.VMEM((SSD_HEADS, SSD_STATE, SSD_HEAD_DIM), F32),
                        pltpu.VMEM((L, SSD_WIDTH), F32)],
        compiler_params=_cparams(2),
    )(h, h, h, conv_w.astype(F32), conv_b.astype(F32).reshape(1, SSD_XBC), dtb, a, dexp,
      norm_g.astype(F32).reshape(1, SSD_WIDTH))


S5_T = 16
S5_SG_GROUPS = 128 // S5_GROUP
S5_SGS = S5_GROUPS // S5_SG_GROUPS
S5_LANES = S5_T * 128
S5_SG_STATE = S5_SG_GROUPS * S5_STATE


def _s5_tables(lam_re, lam_im, log_dt, b_re, b_im, c_re, c_im, d_skip):
    f32 = F32
    hp = lax.Precision.HIGHEST
    dt = jnp.exp(log_dt.astype(f32))[:, None]
    lr, li = lam_re.astype(f32), lam_im.astype(f32)
    mag = jnp.exp(lr * dt)
    ar, ai = mag * jnp.cos(li * dt), mag * jnp.sin(li * dt)
    den = lr * lr + li * li
    nr, ni = ar - 1.0, ai
    fr, fi = (nr * lr + ni * li) / den, (ni * lr - nr * li) / den
    bbr = fr[..., None] * b_re - fi[..., None] * b_im
    bbi = fr[..., None] * b_im + fi[..., None] * b_re
    taus = jnp.arange(S5_T + 1, dtype=f32)[:, None, None]
    pmag = jnp.exp(taus * (lr * dt)[None])
    pr, pi = pmag * jnp.cos(taus * (li * dt)[None]), pmag * jnp.sin(taus * (li * dt)[None])
    lbr = pr[..., None] * bbr[None] - pi[..., None] * bbi[None]
    lbi = pr[..., None] * bbi[None] + pi[..., None] * bbr[None]
    cr, ci = c_re.astype(f32), c_im.astype(f32)
    kmat = (jnp.einsum('ghp,tgpk->tghk', cr, lbr, precision=hp)
            - jnp.einsum('ghp,tgpk->tghk', ci, lbi, precision=hp))
    ng, gw = S5_SG_GROUPS, S5_GROUP
    row_g = jnp.arange(128) // gw
    kt = jnp.swapaxes(kmat[:S5_T], 2, 3).reshape(S5_T, S5_SGS, 128, gw)
    bd = jnp.tile(kt, (1, 1, 1, ng)) * (row_g[:, None] == row_g[None, :])
    bd = jnp.swapaxes(bd, 0, 1)
    st_g = jnp.arange(S5_SG_STATE) // S5_STATE
    wmask = row_g[:, None] == st_g[None, :]

    def w_part(lb):
        x = jnp.swapaxes(lb[S5_T - 1::-1][:S5_T], 2, 3)
        x = x.reshape(S5_T, S5_SGS, 128, S5_STATE)
        x = jnp.tile(x, (1, 1, 1, ng)) * wmask
        return jnp.swapaxes(x, 0, 1).reshape(S5_SGS, S5_LANES, S5_SG_STATE)

    wm = jnp.concatenate([w_part(lbr), w_part(lbi)], axis=2)
    pr1, pi1 = pr[1:], pi[1:]
    vr = cr[None] * pr1[:, :, None, :] - ci[None] * pi1[:, :, None, :]
    vi = cr[None] * pi1[:, :, None, :] + ci[None] * pr1[:, :, None, :]

    def v_part(v):
        x = jnp.transpose(v, (1, 3, 0, 2)).reshape(S5_SGS, S5_SG_STATE, S5_T, gw)
        x = jnp.tile(x, (1, 1, 1, ng)) * (st_g[:, None, None] == row_g[None, None, :])
        return x.reshape(S5_SGS, S5_SG_STATE, S5_LANES)

    vm = jnp.concatenate([v_part(vr), -v_part(vi)], axis=1)
    mt = bd
    lam_r = pr[S5_T].reshape(S5_SGS, 1, S5_SG_STATE)
    lam_i = pi[S5_T].reshape(S5_SGS, 1, S5_SG_STATE)
    dexp = jnp.tile(d_skip.astype(f32).reshape(S5_SGS, 1, 1, 128), (1, 1, S5_T, 1)).reshape(S5_SGS, 1, S5_LANES)
    return mt.astype(BF16), wm.astype(BF16), vm.astype(BF16), lam_r, lam_i, dexp


def _s5_kernel(u_ref, bd_ref, w_ref, v_ref, ar_ref, ai_ref, d_ref, o_ref,
               zr_ref, zi_ref, sr_ref, si_ref, cr_ref, ci_ref, mt_ref, *, rows):
    @pl.when(pl.program_id(2) == 0)
    def _():
        cr_ref[...] = jnp.zeros(cr_ref.shape, F32)
        ci_ref[...] = jnp.zeros(ci_ref.shape, F32)

    @pl.when(jnp.logical_and(pl.program_id(1) == 0, pl.program_id(2) == 0))
    def _():
        zero = jnp.zeros((128, 128), BF16)
        for s in range(S5_T):
            for t in range(S5_T):
                mt_ref[s * 128:(s + 1) * 128, t * 128:(t + 1) * 128] = bd_ref[t - s] if t >= s else zero

    u = jnp.concatenate([u_ref[pl.ds(t, rows, stride=S5_T), :] for t in range(S5_T)], axis=1)
    ub = u.astype(BF16)

    zz = jnp.dot(ub, w_ref[...], preferred_element_type=F32)
    zr_ref[...] = zz[:, :S5_SG_STATE]
    zi_ref[...] = zz[:, S5_SG_STATE:]

    ar, ai = ar_ref[...], ai_ref[...]

    def step(r, carry):
        sr, si = carry
        sr_ref[pl.ds(r, 1), :] = sr
        si_ref[pl.ds(r, 1), :] = si
        zr = zr_ref[pl.ds(r, 1), :]
        zi = zi_ref[pl.ds(r, 1), :]
        return ar * sr - ai * si + zr, ar * si + ai * sr + zi

    sr, si = lax.fori_loop(0, rows, step, (cr_ref[...], ci_ref[...]))
    cr_ref[...] = sr
    ci_ref[...] = si

    sin = jnp.concatenate([sr_ref[...], si_ref[...]], axis=1).astype(BF16)
    yoff = jnp.dot(sin, v_ref[...], preferred_element_type=F32)
    tile = 256
    for ct in range(S5_LANES // tile):
        cols = slice(ct * tile, (ct + 1) * tile)
        kdim = (ct + 1) * tile
        y = jnp.dot(ub[:, :kdim], mt_ref[:kdim, cols], preferred_element_type=F32)
        y = jax.nn.gelu(y + yoff[:, cols] + u[:, cols] * d_ref[:, cols])
        for t in range(ct * tile // 128, (ct + 1) * tile // 128):
            o_ref[pl.ds(t, rows, stride=S5_T), :] = y[:, t * 128 - ct * tile:(t + 1) * 128 - ct * tile]


def _s5(h, tables, bsz, seq, rows):
    mt, wm, vm, lar, lai, dexp = tables
    n = bsz * seq
    nch = seq // S5_T
    rows = min(rows, nch)
    nt = nch // rows
    uw, uidx = _packed_block('s5_u')
    ublk = uidx * (uw // 128)

    def per_sg(shape):
        return pl.BlockSpec((None,) + shape, lambda a, b, i: (a,) + (0,) * len(shape))

    kern = functools.partial(_s5_kernel, rows=rows)
    return pl.pallas_call(
        kern,
        grid=(S5_SGS, bsz, nt),
        in_specs=[pl.BlockSpec((rows * S5_T, 128), lambda a, b, i: (b * nt + i, ublk + a)),
                  per_sg((S5_T, 128, 128)), per_sg((S5_LANES, 2 * S5_SG_STATE)),
                  per_sg((2 * S5_SG_STATE, S5_LANES)), per_sg((1, S5_SG_STATE)), per_sg((1, S5_SG_STATE)),
                  per_sg((1, S5_LANES))],
        out_specs=pl.BlockSpec((rows * S5_T, 128), lambda a, b, i: (b * nt + i, a)),
        out_shape=jax.ShapeDtypeStruct((n, S5_WIDTH), F32),
        scratch_shapes=[pltpu.VMEM((rows, S5_SG_STATE), F32), pltpu.VMEM((rows, S5_SG_STATE), F32),
                        pltpu.VMEM((rows, S5_SG_STATE), F32), pltpu.VMEM((rows, S5_SG_STATE), F32),
                        pltpu.VMEM((1, S5_SG_STATE), F32), pltpu.VMEM((1, S5_SG_STATE), F32),
                        pltpu.VMEM((S5_LANES, S5_LANES), BF16)],
        compiler_params=_cparams(3),
    )(h, mt, wm, vm, lar, lai, dexp)


def _mem_kernel(q_ref, z_ref, mk_ref, mv_ref, o_ref):
    scale = HEAD_DIM ** -0.5
    mk = mk_ref[...].astype(BF16)
    mv = mv_ref[...].astype(BF16)
    outs = []
    for h in range(MEM_HEADS):
        sl = slice(h * HEAD_DIM, (h + 1) * HEAD_DIM)
        qh = (q_ref[:, sl] * scale).astype(BF16)
        lg = lax.dot_general(qh, mk[:, sl], NT_DIMS, preferred_element_type=F32)
        m = jnp.max(lg, axis=-1, keepdims=True)
        p = jnp.exp(lg - m)
        p = p * (1.0 / jnp.sum(p, axis=-1, keepdims=True))
        outs.append(jnp.dot(p.astype(BF16), mv[:, sl], preferred_element_type=F32))
    o_ref[...] = (jnp.concatenate(outs, axis=1) * _silu(z_ref[...])).astype(o_ref.dtype)


def _mem_attn(h, memkv, bsz, seq, t=512):
    n = bsz * seq
    t = min(t, seq)
    nt = seq // t
    qw, qidx = _packed_block('mem_q')
    zw, zidx = _packed_block('mem_z')
    return pl.pallas_call(
        _mem_kernel,
        grid=(bsz, nt),
        in_specs=[pl.BlockSpec((t, qw), lambda b, i: (b * nt + i, qidx)),
                  pl.BlockSpec((t, zw), lambda b, i: (b * nt + i, zidx)),
                  pl.BlockSpec((MEM_LEN, MEM_WIDTH), lambda b, i: (b, 0)),
                  pl.BlockSpec((MEM_LEN, MEM_WIDTH), lambda b, i: (b, 1))],
        out_specs=pl.BlockSpec((t, MEM_WIDTH), lambda b, i: (b * nt + i, 0)),
        out_shape=jax.ShapeDtypeStruct((n, MEM_WIDTH), BF16),
        compiler_params=_cparams(2),
    )(h, h, memkv, memkv)


def _merge_kernel(x_ref, s5_ref, s5z_ref, att_ref, ssd_ref, mem_ref,
                  wglu_ref, bglu_ref, wg_ref, bg_ref, ws5_ref, watt_ref, wssd_ref, wmem_ref,
                  wout_ref, lng_ref, lnb_ref, o_ref, xb_ref, ys5_ref, acc_ref):
    nidx = pl.program_id(1)

    @pl.when(nidx == 0)
    def _():
        xb_ref[...] = x_ref[...].astype(BF16)
        y = s5_ref[...]
        glu = y * _sigmoid(jnp.dot(y.astype(BF16), wglu_ref[...], preferred_element_type=F32) + bglu_ref[...])
        ys5_ref[...] = (glu * _silu(s5z_ref[...])).astype(BF16)
        acc_ref[...] = jnp.zeros(acc_ref.shape, F32)

    xb = xb_ref[...]
    branches = ((ys5_ref, ws5_ref), (att_ref, watt_ref), (ssd_ref, wssd_ref), (mem_ref, wmem_ref))
    merged = None
    for i, (y_ref, w_ref) in enumerate(branches):
        gate = _sigmoid(jnp.dot(xb, wg_ref[i], preferred_element_type=F32) + bg_ref[i])
        term = gate * jnp.dot(y_ref[...], w_ref[...], preferred_element_type=F32)
        merged = term if merged is None else merged + term
    acc_ref[...] += jnp.dot(merged.astype(BF16), wout_ref[...], preferred_element_type=F32)

    @pl.when(nidx == pl.num_programs(1) - 1)
    def _():
        r = DEEPNORM_ALPHA * x_ref[...] + acc_ref[...]
        mu = jnp.mean(r, axis=-1, keepdims=True)
        rc = r - mu
        var = jnp.mean(rc * rc, axis=-1, keepdims=True)
        o_ref[...] = rc * lax.rsqrt(var + LN_EPS) * lng_ref[...] + lnb_ref[...]


def _merge(x, h, ys5, yatt, yssd, ymem, wglu, bglu, wg4, bg4, ws5, watt, wssd, wmem, wout, lng, lnb,
           tm=512, tn=256):
    n = x.shape[0]
    tm = min(tm, n)
    zw, zidx = _packed_block('s5_z')
    row = lambda w: pl.BlockSpec((tm, w), lambda i, c: (i, 0))
    colw = lambda k: pl.BlockSpec((k, tn), lambda i, c: (0, c))
    const = lambda shape: pl.BlockSpec(shape, lambda i, c: (0,) * len(shape))
    return pl.pallas_call(
        _merge_kernel,
        grid=(n // tm, D_MODEL // tn),
        in_specs=[row(D_MODEL), row(S5_WIDTH), pl.BlockSpec((tm, zw), lambda i, c: (i, zidx)),
                  row(ATTN_WIDTH), row(SSD_WIDTH), row(MEM_WIDTH),
                  const((S5_WIDTH, S5_WIDTH)), const((1, S5_WIDTH)),
                  pl.BlockSpec((N_BRANCH, D_MODEL, tn), lambda i, c: (0, 0, c)),
                  pl.BlockSpec((N_BRANCH, 1, tn), lambda i, c: (0, 0, c)),
                  colw(S5_WIDTH), colw(ATTN_WIDTH), colw(SSD_WIDTH), colw(MEM_WIDTH),
                  pl.BlockSpec((tn, D_MODEL), lambda i, c: (c, 0)),
                  const((1, D_MODEL)), const((1, D_MODEL))],
        out_specs=pl.BlockSpec((tm, D_MODEL), lambda i, c: (i, 0)),
        out_shape=jax.ShapeDtypeStruct((n, D_MODEL), F32),
        scratch_shapes=[pltpu.VMEM((tm, D_MODEL), BF16), pltpu.VMEM((tm, S5_WIDTH), BF16),
                        pltpu.VMEM((tm, D_MODEL), F32)],
        compiler_params=_cparams(2),
    )(x, ys5, h, yatt, yssd, ymem, wglu, bglu, wg4, bg4, ws5, watt, wssd, wmem, wout, lng, lnb)


def _split_w_in(w):
    cols, off = {}, 0
    for name, width in SPLITS:
        cols[name] = w[:, off:off + width]
        off += width
    d = w.shape[0]
    pieces = []
    for name, width in PACKED:
        if name == 'idx_kw':
            pieces += [cols['idx_k'], cols['idx_w'], jnp.zeros((d, width - IDX_DIM - IDX_HEADS), w.dtype)]
        elif name == 'ssd_dt':
            pieces += [cols['ssd_dt'], jnp.zeros((d, width - SSD_HEADS), w.dtype)]
        elif name == 'pad':
            pieces.append(jnp.zeros((d, width), w.dtype))
        else:
            pieces.append(cols[name])
    packed = jnp.concatenate(pieces, axis=1).astype(BF16)
    gates = cols['gates'].reshape(d, N_BRANCH, D_MODEL).transpose(1, 0, 2).astype(BF16)
    return packed, gates


def _rope_tables(positions):
    half = ROPE_DIM // 2
    inv = ROPE_THETA ** (-jnp.arange(half, dtype=F32) * 2.0 / ROPE_DIM)
    ang = positions.astype(F32).reshape(-1)[:, None] * inv
    cos, sin = jnp.cos(ang), jnp.sin(ang)
    n = ang.shape[0]
    ones = jnp.ones((n, HEAD_DIM - ROPE_DIM), F32)
    zeros = jnp.zeros((n, HEAD_DIM - ROPE_DIM), F32)
    zh = jnp.zeros((n, half), F32)
    c64 = jnp.concatenate([cos, cos, ones], axis=1)
    sp64 = jnp.concatenate([zh, sin, zeros], axis=1)
    sm64 = jnp.concatenate([-sin, zh, zeros], axis=1)
    dup = lambda t: jnp.concatenate([t, t], axis=1)
    return dup(c64), dup(sp64), dup(sm64)


def kernel(x, mem, positions, w_in, b_gate, s5_lam_re, s5_lam_im, s5_log_dt, s5_b_re, s5_b_im, s5_c_re, s5_c_im, s5_d, s5_w_glu, s5_b_glu, ssd_conv_w, ssd_conv_b, ssd_dt_bias, ssd_a_log, ssd_d, ssd_norm_g, mem_w_kv, w_br_s5, w_br_attn, w_br_ssd, w_br_mem, w_out, ln_g, ln_b):
    bsz, seq, d = x.shape
    n = bsz * seq
    depth = w_in.shape[0]
    qb = min(256, seq)
    nq = seq // qb
    ctab, sptab, smtab = _rope_tables(positions)
    xf = x.reshape(n, d).astype(F32)
    memf = mem.reshape(bsz * MEM_LEN, d).astype(F32)

    for i in range(depth):
        w_packed, w_gates = _split_w_in(w_in[i])
        h = _matmul(xf, w_packed, 1024, 512)
        memkv = _matmul(memf, mem_w_kv[i].astype(BF16), 512, 512)

        tables = _s5_tables(s5_lam_re[i], s5_lam_im[i], s5_log_dt[i], s5_b_re[i], s5_b_im[i],
                            s5_c_re[i], s5_c_im[i], s5_d[i])
        ys5 = _s5(h, tables, bsz, seq, 256)

        qp, iqp, kr, vt4, ki, wt = _prep(h, ctab, sptab, smtab, bsz, seq, qb)
        yatt = _dsa(h, qp, iqp, wt, ki, kr, vt4, bsz, seq, qb)

        yssd = _ssd(h, ssd_conv_w[i], ssd_conv_b[i], ssd_dt_bias[i], ssd_a_log[i], ssd_d[i],
                    ssd_norm_g[i], bsz, seq)

        ymem = _mem_attn(h, memkv, bsz, seq)

        xf = _merge(xf, h, ys5, yatt, yssd, ymem,
                    s5_w_glu[i].astype(BF16), s5_b_glu[i].astype(F32).reshape(1, S5_WIDTH),
                    w_gates, b_gate[i].astype(F32).reshape(N_BRANCH, 1, D_MODEL),
                    w_br_s5[i].astype(BF16), w_br_attn[i].astype(BF16), w_br_ssd[i].astype(BF16),
                    w_br_mem[i].astype(BF16), w_out[i].astype(BF16),
                    ln_g[i].astype(F32).reshape(1, D_MODEL), ln_b[i].astype(F32).reshape(1, D_MODEL))
    return xf.reshape(bsz, seq, d).astype(x.dtype)
```

```python
import functools
import math

import jax
import jax.numpy as jnp
from jax import lax
from jax.experimental import pallas as pl
from jax.experimental.pallas import tpu as pltpu

F32 = jnp.float32
BF16 = jnp.bfloat16

D_MODEL = 2048
S5_WIDTH = 512
S5_GROUP = 16
S5_GROUPS = 32
S5_STATE = 64
HEAD_DIM = 64
ATTN_HEADS = 8
ATTN_KV_HEADS = 2
ATTN_WIDTH = 512
IDX_HEADS = 4
IDX_DIM = 64
TOPK_MAX = 256
SSD_WIDTH = 1024
SSD_HEAD_DIM = 64
SSD_HEADS = 16
SSD_GROUPS = 2
SSD_STATE = 128
SSD_CONV = 4
SSD_CHUNK = 128
SSD_XBC = 1536
MEM_LEN = 256
MEM_HEADS = 4
MEM_WIDTH = 256
N_BRANCH = 4
ROPE_THETA = 500000.0
ROPE_DIM = 16
DEPTH = 2
DEEPNORM_ALPHA = (2 * DEPTH) ** 0.25
LN_EPS = 1e-5
RMS_EPS = 1e-5

SPLITS = (
    ('s5_u', 512), ('s5_z', 512),
    ('att_q', 512), ('att_k', 128), ('att_v', 128), ('att_z', 512),
    ('idx_q', 256), ('idx_k', 64), ('idx_w', 4),
    ('ssd_z', 1024), ('ssd_xbc', 1536), ('ssd_dt', 16),
    ('mem_q', 256), ('mem_z', 256),
    ('gates', 8192),
)

PACKED = (
    ('ssd_z', 1024), ('s5_u', 512), ('ssd_xbc', 1536), ('s5_z', 512), ('att_q', 512), ('att_z', 512),
    ('idx_q', 256), ('mem_q', 256), ('mem_z', 256), ('att_k', 128), ('att_v', 128),
    ('idx_kw', 128), ('ssd_dt', 128), ('pad', 256),
)
PACKED_WIDTH = sum(w for _, w in PACKED)


def _packed_block(name):
    off = 0
    for n, w in PACKED:
        if n == name:
            assert off % w == 0
            return w, off // w
        off += w
    raise KeyError(name)


VMEM_LIMIT = 56 * 1024 * 1024


def _cparams(n_axes, vmem=VMEM_LIMIT):
    return pltpu.CompilerParams(dimension_semantics=("arbitrary",) * n_axes, vmem_limit_bytes=vmem)


def _silu(x):
    return x * (1.0 / (1.0 + jnp.exp(-x)))


def _sigmoid(x):
    return 1.0 / (1.0 + jnp.exp(-x))


def _mm_kernel(x_ref, w_ref, o_ref, xb_ref):
    @pl.when(pl.program_id(1) == 0)
    def _():
        xb_ref[...] = x_ref[...].astype(BF16)

    o_ref[...] = jnp.dot(xb_ref[...], w_ref[...], preferred_element_type=F32).astype(o_ref.dtype)


def _matmul(x, w, tm, tn, out_dtype=F32):
    m, k = x.shape
    n = w.shape[1]
    tm = min(tm, m)
    tn = min(tn, n)
    assert m % tm == 0 and n % tn == 0
    return pl.pallas_call(
        _mm_kernel,
        grid=(m // tm, n // tn),
        in_specs=[pl.BlockSpec((tm, k), lambda i, j: (i, 0)),
                  pl.BlockSpec((k, tn), lambda i, j: (0, j))],
        out_specs=pl.BlockSpec((tm, tn), lambda i, j: (i, j)),
        out_shape=jax.ShapeDtypeStruct((m, n), out_dtype),
        scratch_shapes=[pltpu.VMEM((tm, k), BF16)],
        compiler_params=_cparams(2),
    )(x, w)


def _rope128(x, c, sp, sm):
    return x * c + pltpu.roll(x, 8, 1) * sp + pltpu.roll(x, 120, 1) * sm


def _prep_kernel(q_ref, iq_ref, k_ref, v_ref, kw_ref, c_ref, sp_ref, sm_ref,
                 qo_ref, iqo_ref, ko_ref, vo_ref, kio_ref, wo_ref):
    c, sp, sm = c_ref[...], sp_ref[...], sm_ref[...]
    t = q_ref.shape[0]
    zeros64 = jnp.zeros((t, 64), F32)
    scale = HEAD_DIM ** -0.5 * math.log2(math.e)
    for pair in range(ATTN_HEADS // 2):
        r = _rope128(q_ref[:, pair * 128:(pair + 1) * 128], c, sp, sm) * scale
        for sub in range(2):
            h = pair * 2 + sub
            part = r[:, sub * 64:(sub + 1) * 64]
            if h // (ATTN_HEADS // ATTN_KV_HEADS) == 0:
                full = jnp.concatenate([part, zeros64], axis=1)
            else:
                full = jnp.concatenate([zeros64, part], axis=1)
            qo_ref[h] = full.astype(BF16)
    for pair in range(IDX_HEADS // 2):
        r = _rope128(iq_ref[:, pair * 128:(pair + 1) * 128], c, sp, sm)
        for sub in range(2):
            iqo_ref[pair * 2 + sub] = r[:, sub * 64:(sub + 1) * 64].astype(BF16)
    ko_ref[...] = _rope128(k_ref[...], c, sp, sm).astype(BF16)
    vt = v_ref[...].T
    nchunk, _, _, qb = vo_ref.shape
    tail = jnp.where(lax.broadcasted_iota(jnp.int32, (V_ROWS - HEAD_DIM, qb), 0) == 0, 1.0, 0.0).astype(BF16)
    for ci in range(nchunk):
        for g in range(ATTN_KV_HEADS):
            vo_ref[ci, g, 0:HEAD_DIM, :] = vt[g * HEAD_DIM:(g + 1) * HEAD_DIM, ci * qb:(ci + 1) * qb].astype(BF16)
            vo_ref[ci, g, HEAD_DIM:V_ROWS, :] = tail
    kw = kw_ref[...]
    kio_ref[...] = _rope128(kw, c, sp, sm)[:, :64].astype(BF16)
    wo_ref[...] = kw.T[64:64 + IDX_HEADS, :] * ((IDX_HEADS ** -0.5) * (IDX_DIM ** -0.5))


def _prep(h, ctab, sptab, smtab, bsz, seq, qb, t=512):
    n = bsz * seq
    t = min(t, seq)
    nt = seq // t
    assert t % qb == 0

    def hspec(name):
        w, idx = _packed_block(name)
        return pl.BlockSpec((t, w), lambda b, i: (b * nt + i, idx))

    def rows(w):
        return pl.BlockSpec((t, w), lambda b, i: (b * nt + i, 0))

    return pl.pallas_call(
        _prep_kernel,
        grid=(bsz, nt),
        in_specs=[hspec('att_q'), hspec('idx_q'), hspec('att_k'), hspec('att_v'), hspec('idx_kw'),
                  rows(128), rows(128), rows(128)],
        out_specs=[pl.BlockSpec((ATTN_HEADS, t, 128), lambda b, i: (0, b * nt + i, 0)),
                   pl.BlockSpec((IDX_HEADS, t, 64), lambda b, i: (0, b * nt + i, 0)),
                   rows(128),
                   pl.BlockSpec((None, t // qb, ATTN_KV_HEADS, V_ROWS, qb), lambda b, i: (b, i, 0, 0, 0)),
                   rows(64),
                   pl.BlockSpec((None, IDX_HEADS, t), lambda b, i: (b, 0, i))],
        out_shape=[jax.ShapeDtypeStruct((ATTN_HEADS, n, 128), BF16),
                   jax.ShapeDtypeStruct((IDX_HEADS, n, 64), BF16),
                   jax.ShapeDtypeStruct((n, 128), BF16),
                   jax.ShapeDtypeStruct((bsz, seq // qb, ATTN_KV_HEADS, V_ROWS, qb), BF16),
                   jax.ShapeDtypeStruct((n, 64), BF16),
                   jax.ShapeDtypeStruct((bsz, IDX_HEADS, seq), F32)],
        compiler_params=_cparams(2),
    )(h, h, h, h, h, ctab, sptab, smtab)


BISECT_STEPS = 15
V_ROWS = 80
NT_DIMS = (((1,), (1,)), ((), ()))


def _dsa_kernel(q_ref, iq_ref, wt_ref, az_ref, ki_ref, k_ref, vt_ref, o_ref,
                s_ref, acc_ref, l_ref, lg_ref, *, qb, topk):
    kb = qb
    j = pl.program_id(1)
    q0 = j * qb
    nk = j + 1
    neg = -jnp.inf
    qpos = q0 + lax.broadcasted_iota(jnp.int32, (1, qb), 1)
    row = lax.broadcasted_iota(jnp.int32, (kb, qb), 0)
    wt = wt_ref[...]

    def chunk(c):
        if isinstance(c, int):
            return pl.ds(c * kb, kb)
        return pl.ds(pl.multiple_of(c * kb, kb), kb)

    def p1(c, carry):
        mx, mn = carry
        kic = ki_ref[chunk(c), :]
        acc = jnp.zeros((kb, qb), F32)
        for h in range(IDX_HEADS):
            d = lax.dot_general(kic, iq_ref[h], NT_DIMS, preferred_element_type=F32)
            acc = acc + jnp.maximum(d, 0.0) * wt[h:h + 1, :]
        vis = (row + c * kb) <= qpos
        s_ref[chunk(c), :] = jnp.where(vis, acc, neg)
        mx = jnp.maximum(mx, jnp.max(jnp.where(vis, acc, neg), axis=0, keepdims=True))
        mn = jnp.minimum(mn, jnp.min(jnp.where(vis, acc, jnp.inf), axis=0, keepdims=True))
        return mx, mn

    mx, mn = lax.fori_loop(0, nk, p1, (jnp.full((1, qb), neg, F32), jnp.full((1, qb), jnp.inf, F32)))

    @pl.when(nk % 2 == 1)
    def _():
        s_ref[chunk(nk), :] = jnp.full((kb, qb), neg, F32)

    npair = (nk + 1) // 2

    sub = 64

    def sweep(fn, init):
        def body(i, a):
            base = pl.multiple_of(i * (2 * kb), 2 * kb)
            for r in range(0, 2 * kb, sub):
                a = fn(a, s_ref[pl.ds(base + r, sub), :])
            return a
        return lax.fori_loop(0, npair, body, init)

    def count(pred_fn):
        a = sweep(lambda a, s: a + jnp.where(pred_fn(s), 1.0, 0.0).reshape(sub // 8, 8, qb).sum(axis=0),
                  jnp.zeros((8, qb), F32))
        return jnp.sum(a, axis=0, keepdims=True)

    def max_below(hi):
        a = sweep(lambda a, s: jnp.maximum(a, jnp.where(s < hi, s, neg).reshape(sub // 8, 8, qb).max(axis=0)),
                  jnp.full((8, qb), neg, F32))
        return jnp.max(a, axis=0, keepdims=True)

    nvis = (qpos + 1).astype(F32)
    keff = jnp.minimum(nvis, float(topk))

    def bis_cond(carry):
        return jnp.logical_and(carry[0] > 0.0, carry[1] < BISECT_STEPS)

    def bis_body(carry):
        _, it, lo, hi, clo = carry
        top = jnp.minimum(hi, mx)
        mid = lo + 0.5 * (top - lo)
        cnt = count(lambda s: s >= mid)
        ok = cnt >= keff
        lo = jnp.where(ok, mid, lo)
        clo = jnp.where(ok, cnt, clo)
        hi = jnp.where(ok, hi, mid)
        return jnp.sum(jnp.where(clo == keff, 0.0, 1.0)), it + 1, lo, hi, clo

    _, _, lo, hi, clo = lax.while_loop(
        bis_cond, bis_body,
        (jnp.sum(jnp.where(nvis == keff, 0.0, 1.0)), jnp.int32(0), mn, jnp.full((1, qb), jnp.inf, F32), nvis))

    def fin_cond(carry):
        return carry[0] > 0.0

    def fin_body(carry):
        _, hi, thr, done = carry
        cand = max_below(hi)
        ok = count(lambda s: s >= cand) >= keff
        newly = jnp.logical_and(ok, done < 0.5)
        thr = jnp.where(newly, cand, thr)
        hi = jnp.where(jnp.logical_or(ok, done > 0.5), hi, cand)
        done = jnp.where(ok, 1.0, done)
        return jnp.sum(1.0 - done), hi, thr, done

    done0 = jnp.where(clo == keff, 1.0, 0.0)
    _, _, thr, _ = lax.while_loop(fin_cond, fin_body, (jnp.sum(1.0 - done0), hi, lo, done0))
    need = keff - count(lambda s: s > thr)

    l_ref[...] = jnp.zeros(l_ref.shape, F32)
    acc_ref[...] = jnp.zeros(acc_ref.shape, F32)
    tril = jnp.where(lax.broadcasted_iota(jnp.int32, (kb, kb), 1) <= lax.broadcasted_iota(jnp.int32, (kb, kb), 0),
                     1.0, 0.0).astype(BF16)

    def stage1(c, run, m_old):
        s = s_ref[chunk(c), :]
        eq = jnp.where(s == thr, 1.0, 0.0)
        rank = jnp.dot(tril, eq.astype(BF16), preferred_element_type=F32) + run
        tie_ok = jnp.where(s == thr, rank, jnp.inf) <= need
        bias = jnp.where(s > thr, 0.0, jnp.where(tie_ok, 0.0, neg))
        kc = k_ref[chunk(c), :]
        slot = c % 2
        cms = []
        for h in range(ATTN_HEADS):
            lg = lax.dot_general(kc, q_ref[h], NT_DIMS, preferred_element_type=F32) + bias
            lg_ref[slot, h] = lg
            cms.append(jnp.max(lg.reshape(kb // 8, 8, qb).max(axis=0), axis=0, keepdims=True))
        m_new = jnp.maximum(m_old, jnp.concatenate(cms, axis=0))
        m_safe = jnp.where(m_new == neg, 0.0, m_new)
        alpha = jnp.exp2(m_old - m_safe)
        return rank[kb - 1:kb, :], m_new, m_safe, alpha

    def stage2(c, m_safe, alpha):
        slot = c % 2
        lsum = []
        for h in range(ATTN_HEADS):
            g = h // (ATTN_HEADS // ATTN_KV_HEADS)
            p = jnp.exp2(lg_ref[slot, h] - m_safe[h:h + 1, :])
            pv = jnp.dot(vt_ref[c, g], p.astype(BF16), preferred_element_type=F32)
            lsum.append(pv[HEAD_DIM:HEAD_DIM + 1, :])
            acc_ref[h * 64:(h + 1) * 64, :] = alpha[h:h + 1, :] * acc_ref[h * 64:(h + 1) * 64, :] + pv[:HEAD_DIM, :]
        l_ref[...] = alpha * l_ref[...] + jnp.concatenate(lsum, axis=0)

    def p3(c, carry):
        run, m_old, m_safe, alpha = carry
        stage2(c - 1, m_safe, alpha)
        return stage1(c, run, m_old)

    carry = stage1(0, jnp.zeros((1, qb), F32), jnp.full((ATTN_HEADS, qb), neg, F32))
    _, _, m_safe, alpha = lax.fori_loop(1, nk, p3, carry)
    stage2(nk - 1, m_safe, alpha)

    for h in range(ATTN_HEADS):
        acc_ref[h * 64:(h + 1) * 64, :] = acc_ref[h * 64:(h + 1) * 64, :] * (1.0 / l_ref[h:h + 1, :])
    o_ref[...] = (acc_ref[...].T * _silu(az_ref[...])).astype(o_ref.dtype)


def _dsa(h, qp, iqp, wt, ki, kr, vt4, bsz, seq, qb):
    n = bsz * seq
    nq = seq // qb
    topk = min(TOPK_MAX, seq // 4)
    zw, zidx = _packed_block('att_z')
    kern = functools.partial(_dsa_kernel, qb=qb, topk=topk)
    return pl.pallas_call(
        kern,
        grid=(bsz, nq),
        in_specs=[pl.BlockSpec((ATTN_HEADS, qb, 128), lambda b, j: (0, b * nq + j, 0)),
                  pl.BlockSpec((IDX_HEADS, qb, 64), lambda b, j: (0, b * nq + j, 0)),
                  pl.BlockSpec((None, IDX_HEADS, qb), lambda b, j: (b, 0, j)),
                  pl.BlockSpec((qb, zw), lambda b, j: (b * nq + j, zidx)),
                  pl.BlockSpec((seq, 64), lambda b, j: (b, 0)),
                  pl.BlockSpec((seq, 128), lambda b, j: (b, 0)),
                  pl.BlockSpec((None, nq, ATTN_KV_HEADS, V_ROWS, qb), lambda b, j: (b, 0, 0, 0, 0))],
        out_specs=pl.BlockSpec((qb, ATTN_WIDTH), lambda b, j: (b * nq + j, 0)),
        out_shape=jax.ShapeDtypeStruct((n, ATTN_WIDTH), BF16),
        scratch_shapes=[pltpu.VMEM((seq + qb, qb), F32),
                        pltpu.VMEM((ATTN_WIDTH, qb), F32),
                        pltpu.VMEM((ATTN_HEADS, qb), F32),
                        pltpu.VMEM((2, ATTN_HEADS, qb, qb), F32)],
        compiler_params=_cparams(2),
    )(qp, iqp, wt, h, ki, kr, vt4)


def _ssd_kernel(z_ref, xbc_ref, dt_ref, cw_ref, cb_ref, dtb_ref, a_ref, d_ref, ng_ref, o_ref,
                xc_ref, st_ref, y_ref):
    c = pl.program_id(1)
    L = SSD_CHUNK

    @pl.when(c == 0)
    def _():
        xc_ref[0:8, :] = jnp.zeros((8, SSD_XBC), F32)
        st_ref[...] = jnp.zeros(st_ref.shape, F32)

    xc_ref[8:8 + L, :] = xbc_ref[...]
    conv = cb_ref[...] + cw_ref[3:4, :] * xc_ref[8:8 + L, :]
    for k in range(1, SSD_CONV):
        conv = conv + cw_ref[3 - k:4 - k, :] * xc_ref[8 - k:8 - k + L, :]
    xc_ref[0:8, :] = xc_ref[L:L + 8, :]
    xbc = _silu(conv)

    dtr = dt_ref[...] + dtb_ref[...]
    dt = jnp.maximum(dtr, 0.0) + jnp.log1p(jnp.exp(-jnp.abs(dtr)))
    da = dt * a_ref[...]
    ri = lax.broadcasted_iota(jnp.int32, (L, L), 0)
    ci = lax.broadcasted_iota(jnp.int32, (L, L), 1)
    causal = ci <= ri
    tri = jnp.where(causal, 1.0, 0.0)
    acs = jnp.dot(tri, da, preferred_element_type=F32, precision=lax.Precision.HIGHEST)
    acs_t = acs.T
    e_acs = jnp.exp(acs)
    e_last = jnp.exp(acs[L - 1:L, :])
    dec = jnp.exp(acs[L - 1:L, :] - acs)

    hpg = SSD_HEADS // SSD_GROUPS
    for g in range(SSD_GROUPS):
        bm = xbc[:, SSD_WIDTH + g * SSD_STATE:SSD_WIDTH + (g + 1) * SSD_STATE]
        cm = xbc[:, SSD_WIDTH + (SSD_GROUPS + g) * SSD_STATE:SSD_WIDTH + (SSD_GROUPS + g + 1) * SSD_STATE]
        bmb = bm.astype(BF16)
        cmb = cm.astype(BF16)
        cb = lax.dot_general(cmb, bmb, NT_DIMS, preferred_element_type=F32)
        bmt = bm.T.astype(BF16)
        for jj in range(hpg):
            hd = g * hpg + jj
            xj = xbc[:, hd * SSD_HEAD_DIM:(hd + 1) * SSD_HEAD_DIM]
            xdt = xj * dt[:, hd:hd + 1]
            seg = acs[:, hd:hd + 1] - acs_t[hd:hd + 1, :]
            lm = jnp.exp(jnp.where(causal, seg, -jnp.inf))
            y = jnp.dot((cb * lm).astype(BF16), xdt.astype(BF16), preferred_element_type=F32)
            prev = st_ref[hd]
            y = y + jnp.dot(cmb, prev.astype(BF16), preferred_element_type=F32) * e_acs[:, hd:hd + 1]
            st = jnp.dot(bmt, (xdt * dec[:, hd:hd + 1]).astype(BF16), preferred_element_type=F32)
            st_ref[hd] = prev * e_last[:, hd:hd + 1] + st
            y_ref[:, hd * SSD_HEAD_DIM:(hd + 1) * SSD_HEAD_DIM] = y + xj * d_ref[:, hd * SSD_HEAD_DIM:(hd + 1) * SSD_HEAD_DIM]

    y = y_ref[...] * _silu(z_ref[...])
    gw = SSD_WIDTH // SSD_GROUPS
    parts = []
    for g in range(SSD_GROUPS):
        yg = y[:, g * gw:(g + 1) * gw]
        parts.append(yg * lax.rsqrt(jnp.mean(yg * yg, axis=-1, keepdims=True) + RMS_EPS))
    o_ref[...] = (jnp.concatenate(parts, axis=1) * ng_ref[...]).astype(o_ref.dtype)


def _ssd(h, conv_w, conv_b, dt_bias, a_log, d_skip, norm_g, bsz, seq):
    n = bsz * seq
    nc = seq // SSD_CHUNK
    L = SSD_CHUNK

    def hspec(name):
        w, idx = _packed_block(name)
        return pl.BlockSpec((L, w), lambda b, c: (b * nc + c, idx))

    def full(shape):
        return pl.BlockSpec(shape, lambda b, c: (0,) * len(shape))

    pad = 128 - SSD_HEADS
    dtb = jnp.pad(dt_bias.astype(F32), (0, pad)).reshape(1, 128)
    a = jnp.pad(-jnp.exp(a_log.astype(F32)), (0, pad)).reshape(1, 128)
    dexp = jnp.repeat(d_skip.astype(F32), SSD_HEAD_DIM).reshape(1, SSD_WIDTH)
    return pl.pallas_call(
        _ssd_kernel,
        grid=(bsz, nc),
        in_specs=[hspec('ssd_z'), hspec('ssd_xbc'), hspec('ssd_dt'),
                  full((SSD_CONV, SSD_XBC)), full((1, SSD_XBC)), full((1, 128)), full((1, 128)),
                  full((1, SSD_WIDTH)), full((1, SSD_WIDTH))],
        out_specs=pl.BlockSpec((L, SSD_WIDTH), lambda b, c: (b * nc + c, 0)),
        out_shape=jax.ShapeDtypeStruct((n, SSD_WIDTH), BF16),
        scratch_shapes=[pltpu.VMEM((L + 8, SSD_XBC), F32),
                        pltpu.VMEM((SSD_HEADS, SSD_STATE, SSD_HEAD_DIM), F32),
                        pltpu.VMEM((L, SSD_WIDTH), F32)],
        compiler_params=_cparams(2),
    )(h, h, h, conv_w.astype(F32), conv_b.astype(F32).reshape(1, SSD_XBC), dtb, a, dexp,
      norm_g.astype(F32).reshape(1, SSD_WIDTH))


S5_T = 16
S5_SG_GROUPS = 128 // S5_GROUP
S5_SGS = S5_GROUPS // S5_SG_GROUPS
S5_LANES = S5_T * 128
S5_SG_STATE = S5_SG_GROUPS * S5_STATE


def _s5_tables(lam_re, lam_im, log_dt, b_re, b_im, c_re, c_im, d_skip):
    f32 = F32
    hp = lax.Precision.HIGHEST
    dt = jnp.exp(log_dt.astype(f32))[:, None]
    lr, li = lam_re.astype(f32), lam_im.astype(f32)
    mag = jnp.exp(lr * dt)
    ar, ai = mag * jnp.cos(li * dt), mag * jnp.sin(li * dt)
    den = lr * lr + li * li
    nr, ni = ar - 1.0, ai
    fr, fi = (nr * lr + ni * li) / den, (ni * lr - nr * li) / den
    bbr = fr[..., None] * b_re - fi[..., None] * b_im
    bbi = fr[..., None] * b_im + fi[..., None] * b_re
    taus = jnp.arange(S5_T + 1, dtype=f32)[:, None, None]
    pmag = jnp.exp(taus * (lr * dt)[None])
    pr, pi = pmag * jnp.cos(taus * (li * dt)[None]), pmag * jnp.sin(taus * (li * dt)[None])
    lbr = pr[..., None] * bbr[None] - pi[..., None] * bbi[None]
    lbi = pr[..., None] * bbi[None] + pi[..., None] * bbr[None]
    cr, ci = c_re.astype(f32), c_im.astype(f32)
    kmat = (jnp.einsum('ghp,tgpk->tghk', cr, lbr, precision=hp)
            - jnp.einsum('ghp,tgpk->tghk', ci, lbi, precision=hp))
    ng, gw = S5_SG_GROUPS, S5_GROUP
    row_g = jnp.arange(128) // gw
    kt = jnp.swapaxes(kmat[:S5_T], 2, 3).reshape(S5_T, S5_SGS, 128, gw)
    bd = jnp.tile(kt, (1, 1, 1, ng)) * (row_g[:, None] == row_g[None, :])
    bd = jnp.swapaxes(bd, 0, 1)
    st_g = jnp.arange(S5_SG_STATE) // S5_STATE
    wmask = row_g[:, None] == st_g[None, :]

    def w_part(lb):
        x = jnp.swapaxes(lb[S5_T - 1::-1][:S5_T], 2, 3)
        x = x.reshape(S5_T, S5_SGS, 128, S5_STATE)
        x = jnp.tile(x, (1, 1, 1, ng)) * wmask
        return jnp.swapaxes(x, 0, 1).reshape(S5_SGS, S5_LANES, S5_SG_STATE)

    wm = jnp.concatenate([w_part(lbr), w_part(lbi)], axis=2)
    pr1, pi1 = pr[1:], pi[1:]
    vr = cr[None] * pr1[:, :, None, :] - ci[None] * pi1[:, :, None, :]
    vi = cr[None] * pi1[:, :, None, :] + ci[None] * pr1[:, :, None, :]

    def v_part(v):
        x = jnp.transpose(v, (1, 3, 0, 2)).reshape(S5_SGS, S5_SG_STATE, S5_T, gw)
        x = jnp.tile(x, (1, 1, 1, ng)) * (st_g[:, None, None] == row_g[None, None, :])
        return x.reshape(S5_SGS, S5_SG_STATE, S5_LANES)

    vm = jnp.concatenate([v_part(vr), -v_part(vi)], axis=1)
    mt = bd
    lam_r = pr[S5_T].reshape(S5_SGS, 1, S5_SG_STATE)
    lam_i = pi[S5_T].reshape(S5_SGS, 1, S5_SG_STATE)
    dexp = jnp.tile(d_skip.astype(f32).reshape(S5_SGS, 1, 1, 128), (1, 1, S5_T, 1)).reshape(S5_SGS, 1, S5_LANES)
    return mt.astype(BF16), wm.astype(BF16), vm.astype(BF16), lam_r, lam_i, dexp


def _s5_kernel(u_ref, bd_ref, w_ref, v_ref, ar_ref, ai_ref, d_ref, o_ref,
               zr_ref, zi_ref, sr_ref, si_ref, cr_ref, ci_ref, mt_ref, *, rows):
    @pl.when(pl.program_id(2) == 0)
    def _():
        cr_ref[...] = jnp.zeros(cr_ref.shape, F32)
        ci_ref[...] = jnp.zeros(ci_ref.shape, F32)

    @pl.when(jnp.logical_and(pl.program_id(1) == 0, pl.program_id(2) == 0))
    def _():
        zero = jnp.zeros((128, 128), BF16)
        for s in range(S5_T):
            for t in range(S5_T):
                mt_ref[s * 128:(s + 1) * 128, t * 128:(t + 1) * 128] = bd_ref[t - s] if t >= s else zero

    u = jnp.concatenate([u_ref[pl.ds(t, rows, stride=S5_T), :] for t in range(S5_T)], axis=1)
    ub = u.astype(BF16)

    zz = jnp.dot(ub, w_ref[...], preferred_element_type=F32)
    zr_ref[...] = zz[:, :S5_SG_STATE]
    zi_ref[...] = zz[:, S5_SG_STATE:]

    ar, ai = ar_ref[...], ai_ref[...]

    def step(r, carry):
        sr, si = carry
        sr_ref[pl.ds(r, 1), :] = sr
        si_ref[pl.ds(r, 1), :] = si
        zr = zr_ref[pl.ds(r, 1), :]
        zi = zi_ref[pl.ds(r, 1), :]
        return ar * sr - ai * si + zr, ar * si + ai * sr + zi

    sr, si = lax.fori_loop(0, rows, step, (cr_ref[...], ci_ref[...]))
    cr_ref[...] = sr
    ci_ref[...] = si

    sin = jnp.concatenate([sr_ref[...], si_ref[...]], axis=1).astype(BF16)
    yoff = jnp.dot(sin, v_ref[...], preferred_element_type=F32)
    tile = 256
    for ct in range(S5_LANES // tile):
        cols = slice(ct * tile, (ct + 1) * tile)
        kdim = (ct + 1) * tile
        y = jnp.dot(ub[:, :kdim], mt_ref[:kdim, cols], preferred_element_type=F32)
        y = jax.nn.gelu(y + yoff[:, cols] + u[:, cols] * d_ref[:, cols])
        for t in range(ct * tile // 128, (ct + 1) * tile // 128):
            o_ref[pl.ds(t, rows, stride=S5_T), :] = y[:, t * 128 - ct * tile:(t + 1) * 128 - ct * tile]


def _s5(h, tables, bsz, seq, rows):
    mt, wm, vm, lar, lai, dexp = tables
    n = bsz * seq
    nch = seq // S5_T
    rows = min(rows, nch)
    nt = nch // rows
    uw, uidx = _packed_block('s5_u')
    ublk = uidx * (uw // 128)

    def per_sg(shape):
        return pl.BlockSpec((None,) + shape, lambda a, b, i: (a,) + (0,) * len(shape))

    kern = functools.partial(_s5_kernel, rows=rows)
    return pl.pallas_call(
        kern,
        grid=(S5_SGS, bsz, nt),
        in_specs=[pl.BlockSpec((rows * S5_T, 128), lambda a, b, i: (b * nt + i, ublk + a)),
                  per_sg((S5_T, 128, 128)), per_sg((S5_LANES, 2 * S5_SG_STATE)),
                  per_sg((2 * S5_SG_STATE, S5_LANES)), per_sg((1, S5_SG_STATE)), per_sg((1, S5_SG_STATE)),
                  per_sg((1, S5_LANES))],
        out_specs=pl.BlockSpec((rows * S5_T, 128), lambda a, b, i: (b * nt + i, a)),
        out_shape=jax.ShapeDtypeStruct((n, S5_WIDTH), F32),
        scratch_shapes=[pltpu.VMEM((rows, S5_SG_STATE), F32), pltpu.VMEM((rows, S5_SG_STATE), F32),
                        pltpu.VMEM((rows, S5_SG_STATE), F32), pltpu.VMEM((rows, S5_SG_STATE), F32),
                        pltpu.VMEM((1, S5_SG_STATE), F32), pltpu.VMEM((1, S5_SG_STATE), F32),
                        pltpu.VMEM((S5_LANES, S5_LANES), BF16)],
        compiler_params=_cparams(3),
    )(h, mt, wm, vm, lar, lai, dexp)


def _mem_kernel(q_ref, z_ref, mk_ref, mv_ref, o_ref):
    scale = HEAD_DIM ** -0.5
    mk = mk_ref[...].astype(BF16)
    mv = mv_ref[...].astype(BF16)
    outs = []
    for h in range(MEM_HEADS):
        sl = slice(h * HEAD_DIM, (h + 1) * HEAD_DIM)
        qh = (q_ref[:, sl] * scale).astype(BF16)
        lg = lax.dot_general(qh, mk[:, sl], NT_DIMS, preferred_element_type=F32)
        m = jnp.max(lg, axis=-1, keepdims=True)
        p = jnp.exp(lg - m)
        p = p * (1.0 / jnp.sum(p, axis=-1, keepdims=True))
        outs.append(jnp.dot(p.astype(BF16), mv[:, sl], preferred_element_type=F32))
    o_ref[...] = (jnp.concatenate(outs, axis=1) * _silu(z_ref[...])).astype(o_ref.dtype)


def _mem_attn(h, memkv, bsz, seq, t=512):
    n = bsz * seq
    t = min(t, seq)
    nt = seq // t
    qw, qidx = _packed_block('mem_q')
    zw, zidx = _packed_block('mem_z')
    return pl.pallas_call(
        _mem_kernel,
        grid=(bsz, nt),
        in_specs=[pl.BlockSpec((t, qw), lambda b, i: (b * nt + i, qidx)),
                  pl.BlockSpec((t, zw), lambda b, i: (b * nt + i, zidx)),
                  pl.BlockSpec((MEM_LEN, MEM_WIDTH), lambda b, i: (b, 0)),
                  pl.BlockSpec((MEM_LEN, MEM_WIDTH), lambda b, i: (b, 1))],
        out_specs=pl.BlockSpec((t, MEM_WIDTH), lambda b, i: (b * nt + i, 0)),
        out_shape=jax.ShapeDtypeStruct((n, MEM_WIDTH), BF16),
        compiler_params=_cparams(2),
    )(h, h, memkv, memkv)


def _merge_kernel(x_ref, s5_ref, s5z_ref, att_ref, ssd_ref, mem_ref,
                  wglu_ref, bglu_ref, wg0_ref, wg1_ref, wg2_ref, wg3_ref, bg_ref,
                  ws5_ref, watt_ref, wssd_ref, wmem_ref,
                  wout_ref, lng_ref, lnb_ref, o_ref, xb_ref, ys5_ref, acc_ref):
    nidx = pl.program_id(1)
    wg_ref = (wg0_ref, wg1_ref, wg2_ref, wg3_ref)

    @pl.when(nidx == 0)
    def _():
        xb_ref[...] = x_ref[...].astype(BF16)
        y = s5_ref[...]
        glu = y * _sigmoid(jnp.dot(y.astype(BF16), wglu_ref[...], preferred_element_type=F32) + bglu_ref[...])
        ys5_ref[...] = (glu * _silu(s5z_ref[...])).astype(BF16)
        acc_ref[...] = jnp.zeros(acc_ref.shape, F32)

    xb = xb_ref[...]
    branches = ((ys5_ref, ws5_ref), (att_ref, watt_ref), (ssd_ref, wssd_ref), (mem_ref, wmem_ref))
    merged = None
    for i, (y_ref, w_ref) in enumerate(branches):
        gate = _sigmoid(jnp.dot(xb, wg_ref[i][...], preferred_element_type=F32) + bg_ref[i])
        term = gate * jnp.dot(y_ref[...], w_ref[...], preferred_element_type=F32)
        merged = term if merged is None else merged + term
    acc_ref[...] += jnp.dot(merged.astype(BF16), wout_ref[...], preferred_element_type=F32)

    @pl.when(nidx == pl.num_programs(1) - 1)
    def _():
        r = DEEPNORM_ALPHA * x_ref[...] + acc_ref[...]
        mu = jnp.mean(r, axis=-1, keepdims=True)
        rc = r - mu
        var = jnp.mean(rc * rc, axis=-1, keepdims=True)
        o_ref[...] = rc * lax.rsqrt(var + LN_EPS) * lng_ref[...] + lnb_ref[...]


def _merge(x, h, ys5, yatt, yssd, ymem, wglu, bglu, wg4, bg4, ws5, watt, wssd, wmem, wout, lng, lnb,
           tm=512, tn=256):
    n = x.shape[0]
    tm = min(tm, n)
    zw, zidx = _packed_block('s5_z')
    row = lambda w: pl.BlockSpec((tm, w), lambda i, c: (i, 0))
    colw = lambda k: pl.BlockSpec((k, tn), lambda i, c: (0, c))
    const = lambda shape: pl.BlockSpec(shape, lambda i, c: (0,) * len(shape))
    gate_w = lambda br: pl.BlockSpec((D_MODEL, tn), lambda i, c: (0, br * (D_MODEL // tn) + c))
    return pl.pallas_call(
        _merge_kernel,
        grid=(n // tm, D_MODEL // tn),
        in_specs=[row(D_MODEL), row(S5_WIDTH), pl.BlockSpec((tm, zw), lambda i, c: (i, zidx)),
                  row(ATTN_WIDTH), row(SSD_WIDTH), row(MEM_WIDTH),
                  const((S5_WIDTH, S5_WIDTH)), const((1, S5_WIDTH)),
                  gate_w(0), gate_w(1), gate_w(2), gate_w(3),
                  pl.BlockSpec((N_BRANCH, 1, tn), lambda i, c: (0, 0, c)),
                  colw(S5_WIDTH), colw(ATTN_WIDTH), colw(SSD_WIDTH), colw(MEM_WIDTH),
                  pl.BlockSpec((tn, D_MODEL), lambda i, c: (c, 0)),
                  const((1, D_MODEL)), const((1, D_MODEL))],
        out_specs=pl.BlockSpec((tm, D_MODEL), lambda i, c: (i, 0)),
        out_shape=jax.ShapeDtypeStruct((n, D_MODEL), F32),
        scratch_shapes=[pltpu.VMEM((tm, D_MODEL), BF16), pltpu.VMEM((tm, S5_WIDTH), BF16),
                        pltpu.VMEM((tm, D_MODEL), F32)],
        compiler_params=_cparams(2),
    )(x, ys5, h, yatt, yssd, ymem, wglu, bglu, wg4, wg4, wg4, wg4, bg4, ws5, watt, wssd, wmem, wout, lng, lnb)


def _split_w_in(w):
    cols, off = {}, 0
    for name, width in SPLITS:
        cols[name] = w[:, off:off + width]
        off += width
    d = w.shape[0]
    pieces = []
    for name, width in PACKED:
        if name == 'idx_kw':
            pieces += [cols['idx_k'], cols['idx_w'], jnp.zeros((d, width - IDX_DIM - IDX_HEADS), w.dtype)]
        elif name == 'ssd_dt':
            pieces += [cols['ssd_dt'], jnp.zeros((d, width - SSD_HEADS), w.dtype)]
        elif name == 'pad':
            pieces.append(jnp.zeros((d, width), w.dtype))
        else:
            pieces.append(cols[name])
    packed = jnp.concatenate(pieces, axis=1).astype(BF16)
    gates = cols['gates'].astype(BF16)
    return packed, gates


def _rope_tables(positions):
    half = ROPE_DIM // 2
    inv = ROPE_THETA ** (-jnp.arange(half, dtype=F32) * 2.0 / ROPE_DIM)
    ang = positions.astype(F32).reshape(-1)[:, None] * inv
    cos, sin = jnp.cos(ang), jnp.sin(ang)
    n = ang.shape[0]
    ones = jnp.ones((n, HEAD_DIM - ROPE_DIM), F32)
    zeros = jnp.zeros((n, HEAD_DIM - ROPE_DIM), F32)
    zh = jnp.zeros((n, half), F32)
    c64 = jnp.concatenate([cos, cos, ones], axis=1)
    sp64 = jnp.concatenate([zh, sin, zeros], axis=1)
    sm64 = jnp.concatenate([-sin, zh, zeros], axis=1)
    dup = lambda t: jnp.concatenate([t, t], axis=1)
    return dup(c64), dup(sp64), dup(sm64)


def kernel(x, mem, positions, w_in, b_gate, s5_lam_re, s5_lam_im, s5_log_dt, s5_b_re, s5_b_im, s5_c_re, s5_c_im, s5_d, s5_w_glu, s5_b_glu, ssd_conv_w, ssd_conv_b, ssd_dt_bias, ssd_a_log, ssd_d, ssd_norm_g, mem_w_kv, w_br_s5, w_br_attn, w_br_ssd, w_br_mem, w_out, ln_g, ln_b):
    bsz, seq, d = x.shape
    n = bsz * seq
    depth = w_in.shape[0]
    qb = min(256, seq)
    nq = seq // qb
    ctab, sptab, smtab = _rope_tables(positions)
    xf = x.reshape(n, d).astype(F32)
    memf = mem.reshape(bsz * MEM_LEN, d).astype(F32)

    for i in range(depth):
        w_packed, w_gates = _split_w_in(w_in[i])
        h = _matmul(xf, w_packed, 1024, 512)
        memkv = _matmul(memf, mem_w_kv[i].astype(BF16), 512, 512)

        tables = _s5_tables(s5_lam_re[i], s5_lam_im[i], s5_log_dt[i], s5_b_re[i], s5_b_im[i],
                            s5_c_re[i], s5_c_im[i], s5_d[i])
        ys5 = _s5(h, tables, bsz, seq, 256)

        qp, iqp, kr, vt4, ki, wt = _prep(h, ctab, sptab, smtab, bsz, seq, qb)
        yatt = _dsa(h, qp, iqp, wt, ki, kr, vt4, bsz, seq, qb)

        yssd = _ssd(h, ssd_conv_w[i], ssd_conv_b[i], ssd_dt_bias[i], ssd_a_log[i], ssd_d[i],
                    ssd_norm_g[i], bsz, seq)

        ymem = _mem_attn(h, memkv, bsz, seq)

        xf = _merge(xf, h, ys5, yatt, yssd, ymem,
                    s5_w_glu[i].astype(BF16), s5_b_glu[i].astype(F32).reshape(1, S5_WIDTH),
                    w_gates, b_gate[i].astype(F32).reshape(N_BRANCH, 1, D_MODEL),
                    w_br_s5[i].astype(BF16), w_br_attn[i].astype(BF16), w_br_ssd[i].astype(BF16),
                    w_br_mem[i].astype(BF16), w_out[i].astype(BF16),
                    ln_g[i].astype(F32).reshape(1, D_MODEL), ln_b[i].astype(F32).reshape(1, D_MODEL))
    return xf.reshape(bsz, seq, d).astype(x.dtype)
```

```python
import functools
import math

import jax
import jax.numpy as jnp
from jax import lax
from jax.experimental import pallas as pl
from jax.experimental.pallas import tpu as pltpu

F32 = jnp.float32
BF16 = jnp.bfloat16

D_MODEL = 2048
S5_WIDTH = 512
S5_GROUP = 16
S5_GROUPS = 32
S5_STATE = 64
HEAD_DIM = 64
ATTN_HEADS = 8
ATTN_KV_HEADS = 2
ATTN_WIDTH = 512
IDX_HEADS = 4
IDX_DIM = 64
TOPK_MAX = 256
SSD_WIDTH = 1024
SSD_HEAD_DIM = 64
SSD_HEADS = 16
SSD_GROUPS = 2
SSD_STATE = 128
SSD_CONV = 4
SSD_CHUNK = 128
SSD_XBC = 1536
MEM_LEN = 256
MEM_HEADS = 4
MEM_WIDTH = 256
N_BRANCH = 4
ROPE_THETA = 500000.0
ROPE_DIM = 16
DEPTH = 2
DEEPNORM_ALPHA = (2 * DEPTH) ** 0.25
LN_EPS = 1e-5
RMS_EPS = 1e-5

SPLITS = (
    ('s5_u', 512), ('s5_z', 512),
    ('att_q', 512), ('att_k', 128), ('att_v', 128), ('att_z', 512),
    ('idx_q', 256), ('idx_k', 64), ('idx_w', 4),
    ('ssd_z', 1024), ('ssd_xbc', 1536), ('ssd_dt', 16),
    ('mem_q', 256), ('mem_z', 256),
    ('gates', 8192),
)

PACKED = (
    ('ssd_z', 1024), ('s5_u', 512), ('ssd_xbc', 1536), ('s5_z', 512), ('att_q', 512), ('att_z', 512),
    ('idx_q', 256), ('mem_q', 256), ('mem_z', 256), ('att_k', 128), ('att_v', 128),
    ('idx_kw', 128), ('ssd_dt', 128), ('pad', 256),
)
PACKED_WIDTH = sum(w for _, w in PACKED)


def _packed_block(name):
    off = 0
    for n, w in PACKED:
        if n == name:
            assert off % w == 0
            return w, off // w
        off += w
    raise KeyError(name)


VMEM_LIMIT = 56 * 1024 * 1024


def _cparams(n_axes, vmem=VMEM_LIMIT):
    return pltpu.CompilerParams(dimension_semantics=("arbitrary",) * n_axes, vmem_limit_bytes=vmem)


def _silu(x):
    return x * (1.0 / (1.0 + jnp.exp(-x)))


def _sigmoid(x):
    return 1.0 / (1.0 + jnp.exp(-x))


def _mm_kernel(x_ref, w_ref, o_ref, xb_ref):
    @pl.when(pl.program_id(1) == 0)
    def _():
        xb_ref[...] = x_ref[...].astype(BF16)

    o_ref[...] = jnp.dot(xb_ref[...], w_ref[...], preferred_element_type=F32).astype(o_ref.dtype)


def _matmul(x, w, tm, tn, out_dtype=F32):
    m, k = x.shape
    n = w.shape[1]
    tm = min(tm, m)
    tn = min(tn, n)
    assert m % tm == 0 and n % tn == 0
    return pl.pallas_call(
        _mm_kernel,
        grid=(m // tm, n // tn),
        in_specs=[pl.BlockSpec((tm, k), lambda i, j: (i, 0)),
                  pl.BlockSpec((k, tn), lambda i, j: (0, j))],
        out_specs=pl.BlockSpec((tm, tn), lambda i, j: (i, j)),
        out_shape=jax.ShapeDtypeStruct((m, n), out_dtype),
        scratch_shapes=[pltpu.VMEM((tm, k), BF16)],
        compiler_params=_cparams(2),
    )(x, w)


def _rope128(x, c, sp, sm):
    return x * c + pltpu.roll(x, 8, 1) * sp + pltpu.roll(x, 120, 1) * sm


def _prep_kernel(q_ref, iq_ref, k_ref, v_ref, kw_ref, c_ref, sp_ref, sm_ref,
                 qo_ref, iqo_ref, ko_ref, vo_ref, kio_ref, wo_ref):
    c, sp, sm = c_ref[...], sp_ref[...], sm_ref[...]
    t = q_ref.shape[0]
    zeros64 = jnp.zeros((t, 64), F32)
    scale = HEAD_DIM ** -0.5 * math.log2(math.e)
    for pair in range(ATTN_HEADS // 2):
        r = _rope128(q_ref[:, pair * 128:(pair + 1) * 128], c, sp, sm) * scale
        for sub in range(2):
            h = pair * 2 + sub
            part = r[:, sub * 64:(sub + 1) * 64]
            if h // (ATTN_HEADS // ATTN_KV_HEADS) == 0:
                full = jnp.concatenate([part, zeros64], axis=1)
            else:
                full = jnp.concatenate([zeros64, part], axis=1)
            qo_ref[h] = full.astype(BF16)
    for pair in range(IDX_HEADS // 2):
        r = _rope128(iq_ref[:, pair * 128:(pair + 1) * 128], c, sp, sm)
        for sub in range(2):
            iqo_ref[pair * 2 + sub] = r[:, sub * 64:(sub + 1) * 64].astype(BF16)
    ko_ref[...] = _rope128(k_ref[...], c, sp, sm).astype(BF16)
    vt = v_ref[...].T
    nchunk, _, _, qb = vo_ref.shape
    tail = jnp.where(lax.broadcasted_iota(jnp.int32, (V_ROWS - HEAD_DIM, qb), 0) == 0, 1.0, 0.0).astype(BF16)
    for ci in range(nchunk):
        for g in range(ATTN_KV_HEADS):
            vo_ref[ci, g, 0:HEAD_DIM, :] = vt[g * HEAD_DIM:(g + 1) * HEAD_DIM, ci * qb:(ci + 1) * qb].astype(BF16)
            vo_ref[ci, g, HEAD_DIM:V_ROWS, :] = tail
    kw = kw_ref[...]
    kio_ref[...] = _rope128(kw, c, sp, sm)[:, :64].astype(BF16)
    wo_ref[...] = kw.T[64:64 + IDX_HEADS, :] * ((IDX_HEADS ** -0.5) * (IDX_DIM ** -0.5))


def _prep(h, ctab, sptab, smtab, bsz, seq, qb, t=512):
    n = bsz * seq
    t = min(t, seq)
    nt = seq // t
    assert t % qb == 0

    def hspec(name):
        w, idx = _packed_block(name)
        return pl.BlockSpec((t, w), lambda b, i: (b * nt + i, idx))

    def rows(w):
        return pl.BlockSpec((t, w), lambda b, i: (b * nt + i, 0))

    return pl.pallas_call(
        _prep_kernel,
        grid=(bsz, nt),
        in_specs=[hspec('att_q'), hspec('idx_q'), hspec('att_k'), hspec('att_v'), hspec('idx_kw'),
                  rows(128), rows(128), rows(128)],
        out_specs=[pl.BlockSpec((ATTN_HEADS, t, 128), lambda b, i: (0, b * nt + i, 0)),
                   pl.BlockSpec((IDX_HEADS, t, 64), lambda b, i: (0, b * nt + i, 0)),
                   rows(128),
                   pl.BlockSpec((None, t // qb, ATTN_KV_HEADS, V_ROWS, qb), lambda b, i: (b, i, 0, 0, 0)),
                   rows(64),
                   pl.BlockSpec((None, IDX_HEADS, t), lambda b, i: (b, 0, i))],
        out_shape=[jax.ShapeDtypeStruct((ATTN_HEADS, n, 128), BF16),
                   jax.ShapeDtypeStruct((IDX_HEADS, n, 64), BF16),
                   jax.ShapeDtypeStruct((n, 128), BF16),
                   jax.ShapeDtypeStruct((bsz, seq // qb, ATTN_KV_HEADS, V_ROWS, qb), BF16),
                   jax.ShapeDtypeStruct((n, 64), BF16),
                   jax.ShapeDtypeStruct((bsz, IDX_HEADS, seq), F32)],
        compiler_params=_cparams(2),
    )(h, h, h, h, h, ctab, sptab, smtab)


BISECT_STEPS = 15
V_ROWS = 80
NT_DIMS = (((1,), (1,)), ((), ()))


def _dsa_kernel(q_ref, iq_ref, wt_ref, az_ref, ki_ref, k_ref, vt_ref, o_ref,
                s_ref, acc_ref, l_ref, lg_ref, *, qb, topk):
    kb = qb
    j = pl.program_id(1)
    q0 = j * qb
    nk = j + 1
    neg = -jnp.inf
    qpos = q0 + lax.broadcasted_iota(jnp.int32, (1, qb), 1)
    row = lax.broadcasted_iota(jnp.int32, (kb, qb), 0)
    wt = wt_ref[...]

    def chunk(c):
        if isinstance(c, int):
            return pl.ds(c * kb, kb)
        return pl.ds(pl.multiple_of(c * kb, kb), kb)

    def p1(c, carry):
        mx, mn = carry
        kic = ki_ref[chunk(c), :]
        acc = jnp.zeros((kb, qb), F32)
        for h in range(IDX_HEADS):
            d = lax.dot_general(kic, iq_ref[h], NT_DIMS, preferred_element_type=F32)
            acc = acc + jnp.maximum(d, 0.0) * wt[h:h + 1, :]
        vis = (row + c * kb) <= qpos
        s_ref[chunk(c), :] = jnp.where(vis, acc, neg)
        mx = jnp.maximum(mx, jnp.max(jnp.where(vis, acc, neg), axis=0, keepdims=True))
        mn = jnp.minimum(mn, jnp.min(jnp.where(vis, acc, jnp.inf), axis=0, keepdims=True))
        return mx, mn

    mx, mn = lax.fori_loop(0, nk, p1, (jnp.full((1, qb), neg, F32), jnp.full((1, qb), jnp.inf, F32)))

    @pl.when(nk % 2 == 1)
    def _():
        s_ref[chunk(nk), :] = jnp.full((kb, qb), neg, F32)

    npair = (nk + 1) // 2

    sub = 64

    def sweep(fn, init):
        def body(i, a):
            base = pl.multiple_of(i * (2 * kb), 2 * kb)
            for r in range(0, 2 * kb, sub):
                a = fn(a, s_ref[pl.ds(base + r, sub), :])
            return a
        return lax.fori_loop(0, npair, body, init)

    def count(pred_fn):
        a = sweep(lambda a, s: a + jnp.where(pred_fn(s), 1.0, 0.0).reshape(sub // 8, 8, qb).sum(axis=0),
                  jnp.zeros((8, qb), F32))
        return jnp.sum(a, axis=0, keepdims=True)

    def max_below(hi):
        a = sweep(lambda a, s: jnp.maximum(a, jnp.where(s < hi, s, neg).reshape(sub // 8, 8, qb).max(axis=0)),
                  jnp.full((8, qb), neg, F32))
        return jnp.max(a, axis=0, keepdims=True)

    nvis = (qpos + 1).astype(F32)
    keff = jnp.minimum(nvis, float(topk))

    def bis_cond(carry):
        return jnp.logical_and(carry[0] > 0.0, carry[1] < BISECT_STEPS)

    def bis_body(carry):
        _, it, lo, hi, clo = carry
        top = jnp.minimum(hi, mx)
        mid = lo + 0.5 * (top - lo)
        cnt = count(lambda s: s >= mid)
        ok = cnt >= keff
        lo = jnp.where(ok, mid, lo)
        clo = jnp.where(ok, cnt, clo)
        hi = jnp.where(ok, hi, mid)
        return jnp.sum(jnp.where(clo == keff, 0.0, 1.0)), it + 1, lo, hi, clo

    _, _, lo, hi, clo = lax.while_loop(
        bis_cond, bis_body,
        (jnp.sum(jnp.where(nvis == keff, 0.0, 1.0)), jnp.int32(0), mn, jnp.full((1, qb), jnp.inf, F32), nvis))

    def fin_cond(carry):
        return carry[0] > 0.0

    def fin_body(carry):
        _, hi, thr, done = carry
        cand = max_below(hi)
        ok = count(lambda s: s >= cand) >= keff
        newly = jnp.logical_and(ok, done < 0.5)
        thr = jnp.where(newly, cand, thr)
        hi = jnp.where(jnp.logical_or(ok, done > 0.5), hi, cand)
        done = jnp.where(ok, 1.0, done)
        return jnp.sum(1.0 - done), hi, thr, done

    done0 = jnp.where(clo == keff, 1.0, 0.0)
    _, _, thr, _ = lax.while_loop(fin_cond, fin_body, (jnp.sum(1.0 - done0), hi, lo, done0))
    need = keff - count(lambda s: s > thr)

    l_ref[...] = jnp.zeros(l_ref.shape, F32)
    acc_ref[...] = jnp.zeros(acc_ref.shape, F32)
    tril = jnp.where(lax.broadcasted_iota(jnp.int32, (kb, kb), 1) <= lax.broadcasted_iota(jnp.int32, (kb, kb), 0),
                     1.0, 0.0).astype(BF16)

    def stage1(c, run, m_old):
        s = s_ref[chunk(c), :]
        eq = jnp.where(s == thr, 1.0, 0.0)
        rank = jnp.dot(tril, eq.astype(BF16), preferred_element_type=F32) + run
        tie_ok = jnp.where(s == thr, rank, jnp.inf) <= need
        bias = jnp.where(s > thr, 0.0, jnp.where(tie_ok, 0.0, neg))
        kc = k_ref[chunk(c), :]
        slot = c % 2
        cms = []
        for h in range(ATTN_HEADS):
            lg = lax.dot_general(kc, q_ref[h], NT_DIMS, preferred_element_type=F32) + bias
            lg_ref[slot, h] = lg
            cms.append(jnp.max(lg.reshape(kb // 8, 8, qb).max(axis=0), axis=0, keepdims=True))
        m_new = jnp.maximum(m_old, jnp.concatenate(cms, axis=0))
        m_safe = jnp.where(m_new == neg, 0.0, m_new)
        alpha = jnp.exp2(m_old - m_safe)
        return rank[kb - 1:kb, :], m_new, m_safe, alpha

    def stage2(c, m_safe, alpha):
        slot = c % 2
        lsum = []
        for h in range(ATTN_HEADS):
            g = h // (ATTN_HEADS // ATTN_KV_HEADS)
            p = jnp.exp2(lg_ref[slot, h] - m_safe[h:h + 1, :])
            pv = jnp.dot(vt_ref[c, g], p.astype(BF16), preferred_element_type=F32)
            lsum.append(pv[HEAD_DIM:HEAD_DIM + 1, :])
            acc_ref[h * 64:(h + 1) * 64, :] = alpha[h:h + 1, :] * acc_ref[h * 64:(h + 1) * 64, :] + pv[:HEAD_DIM, :]
        l_ref[...] = alpha * l_ref[...] + jnp.concatenate(lsum, axis=0)

    def p3(c, carry):
        run, m_old, m_safe, alpha = carry
        stage2(c - 1, m_safe, alpha)
        return stage1(c, run, m_old)

    carry = stage1(0, jnp.zeros((1, qb), F32), jnp.full((ATTN_HEADS, qb), neg, F32))
    _, _, m_safe, alpha = lax.fori_loop(1, nk, p3, carry)
    stage2(nk - 1, m_safe, alpha)

    for h in range(ATTN_HEADS):
        acc_ref[h * 64:(h + 1) * 64, :] = acc_ref[h * 64:(h + 1) * 64, :] * (1.0 / l_ref[h:h + 1, :])
    o_ref[...] = (acc_ref[...].T * _silu(az_ref[...])).astype(o_ref.dtype)


def _dsa(h, qp, iqp, wt, ki, kr, vt4, bsz, seq, qb):
    n = bsz * seq
    nq = seq // qb
    topk = min(TOPK_MAX, seq // 4)
    zw, zidx = _packed_block('att_z')
    kern = functools.partial(_dsa_kernel, qb=qb, topk=topk)
    return pl.pallas_call(
        kern,
        grid=(bsz, nq),
        in_specs=[pl.BlockSpec((ATTN_HEADS, qb, 128), lambda b, j: (0, b * nq + j, 0)),
                  pl.BlockSpec((IDX_HEADS, qb, 64), lambda b, j: (0, b * nq + j, 0)),
                  pl.BlockSpec((None, IDX_HEADS, qb), lambda b, j: (b, 0, j)),
                  pl.BlockSpec((qb, zw), lambda b, j: (b * nq + j, zidx)),
                  pl.BlockSpec((seq, 64), lambda b, j: (b, 0)),
                  pl.BlockSpec((seq, 128), lambda b, j: (b, 0)),
                  pl.BlockSpec((None, nq, ATTN_KV_HEADS, V_ROWS, qb), lambda b, j: (b, 0, 0, 0, 0))],
        out_specs=pl.BlockSpec((qb, ATTN_WIDTH), lambda b, j: (b * nq + j, 0)),
        out_shape=jax.ShapeDtypeStruct((n, ATTN_WIDTH), BF16),
        scratch_shapes=[pltpu.VMEM((seq + qb, qb), F32),
                        pltpu.VMEM((ATTN_WIDTH, qb), F32),
                        pltpu.VMEM((ATTN_HEADS, qb), F32),
                        pltpu.VMEM((2, ATTN_HEADS, qb, qb), F32)],
        compiler_params=_cparams(2),
    )(qp, iqp, wt, h, ki, kr, vt4)


def _ssd_kernel(z_ref, xbc_ref, dt_ref, cw_ref, cb_ref, dtb_ref, a_ref, d_ref, ng_ref, o_ref,
                xc_ref, st_ref, y_ref):
    c = pl.program_id(1)
    L = SSD_CHUNK

    @pl.when(c == 0)
    def _():
        xc_ref[0:8, :] = jnp.zeros((8, SSD_XBC), F32)
        st_ref[...] = jnp.zeros(st_ref.shape, F32)

    xc_ref[8:8 + L, :] = xbc_ref[...]
    conv = cb_ref[...] + cw_ref[3:4, :] * xc_ref[8:8 + L, :]
    for k in range(1, SSD_CONV):
        conv = conv + cw_ref[3 - k:4 - k, :] * xc_ref[8 - k:8 - k + L, :]
    xc_ref[0:8, :] = xc_ref[L:L + 8, :]
    xbc = _silu(conv)

    dtr = dt_ref[...] + dtb_ref[...]
    dt = jnp.maximum(dtr, 0.0) + jnp.log1p(jnp.exp(-jnp.abs(dtr)))
    da = dt * a_ref[...]
    ri = lax.broadcasted_iota(jnp.int32, (L, L), 0)
    ci = lax.broadcasted_iota(jnp.int32, (L, L), 1)
    causal = ci <= ri
    tri = jnp.where(causal, 1.0, 0.0)
    acs = jnp.dot(tri, da, preferred_element_type=F32, precision=lax.Precision.HIGHEST)
    acs_t = acs.T
    e_acs = jnp.exp(acs)
    e_last = jnp.exp(acs[L - 1:L, :])
    dec = jnp.exp(acs[L - 1:L, :] - acs)

    hpg = SSD_HEADS // SSD_GROUPS
    for g in range(SSD_GROUPS):
        bm = xbc[:, SSD_WIDTH + g * SSD_STATE:SSD_WIDTH + (g + 1) * SSD_STATE]
        cm = xbc[:, SSD_WIDTH + (SSD_GROUPS + g) * SSD_STATE:SSD_WIDTH + (SSD_GROUPS + g + 1) * SSD_STATE]
        bmb = bm.astype(BF16)
        cmb = cm.astype(BF16)
        cb = lax.dot_general(cmb, bmb, NT_DIMS, preferred_element_type=F32)
        bmt = bm.T.astype(BF16)
        for jj in range(hpg):
            hd = g * hpg + jj
            xj = xbc[:, hd * SSD_HEAD_DIM:(hd + 1) * SSD_HEAD_DIM]
            xdt = xj * dt[:, hd:hd + 1]
            seg = acs[:, hd:hd + 1] - acs_t[hd:hd + 1, :]
            lm = jnp.exp(jnp.where(causal, seg, -jnp.inf))
            y = jnp.dot((cb * lm).astype(BF16), xdt.astype(BF16), preferred_element_type=F32)
            prev = st_ref[hd]
            y = y + jnp.dot(cmb, prev.astype(BF16), preferred_element_type=F32) * e_acs[:, hd:hd + 1]
            st = jnp.dot(bmt, (xdt * dec[:, hd:hd + 1]).astype(BF16), preferred_element_type=F32)
            st_ref[hd] = prev * e_last[:, hd:hd + 1] + st
            y_ref[:, hd * SSD_HEAD_DIM:(hd + 1) * SSD_HEAD_DIM] = y + xj * d_ref[:, hd * SSD_HEAD_DIM:(hd + 1) * SSD_HEAD_DIM]

    y = y_ref[...] * _silu(z_ref[...])
    gw = SSD_WIDTH // SSD_GROUPS
    parts = []
    for g in range(SSD_GROUPS):
        yg = y[:, g * gw:(g + 1) * gw]
        parts.append(yg * lax.rsqrt(jnp.mean(yg * yg, axis=-1, keepdims=True) + RMS_EPS))
    o_ref[...] = (jnp.concatenate(parts, axis=1) * ng_ref[...]).astype(o_ref.dtype)


def _ssd(h, conv_w, conv_b, dt_bias, a_log, d_skip, norm_g, bsz, seq):
    n = bsz * seq
    nc = seq // SSD_CHUNK
    L = SSD_CHUNK

    def hspec(name):
        w, idx = _packed_block(name)
        return pl.BlockSpec((L, w), lambda b, c: (b * nc + c, idx))

    def full(shape):
        return pl.BlockSpec(shape, lambda b, c: (0,) * len(shape))

    pad = 128 - SSD_HEADS
    dtb = jnp.pad(dt_bias.astype(F32), (0, pad)).reshape(1, 128)
    a = jnp.pad(-jnp.exp(a_log.astype(F32)), (0, pad)).reshape(1, 128)
    dexp = jnp.repeat(d_skip.astype(F32), SSD_HEAD_DIM).reshape(1, SSD_WIDTH)
    return pl.pallas_call(
        _ssd_kernel,
        grid=(bsz, nc),
        in_specs=[hspec('ssd_z'), hspec('ssd_xbc'), hspec('ssd_dt'),
                  full((SSD_CONV, SSD_XBC)), full((1, SSD_XBC)), full((1, 128)), full((1, 128)),
                  full((1, SSD_WIDTH)), full((1, SSD_WIDTH))],
        out_specs=pl.BlockSpec((L, SSD_WIDTH), lambda b, c: (b * nc + c, 0)),
        out_shape=jax.ShapeDtypeStruct((n, SSD_WIDTH), BF16),
        scratch_shapes=[pltpu.VMEM((L + 8, SSD_XBC), F32),
                        pltpu.VMEM((SSD_HEADS, SSD_STATE, SSD_HEAD_DIM), F32),
                        pltpu.VMEM((L, SSD_WIDTH), F32)],
        compiler_params=_cparams(2),
    )(h, h, h, conv_w.astype(F32), conv_b.astype(F32).reshape(1, SSD_XBC), dtb, a, dexp,
      norm_g.astype(F32).reshape(1, SSD_WIDTH))


S5_T = 16
S5_SG_GROUPS = 128 // S5_GROUP
S5_SGS = S5_GROUPS // S5_SG_GROUPS
S5_LANES = S5_T * 128
S5_SG_STATE = S5_SG_GROUPS * S5_STATE


def _s5_tables(lam_re, lam_im, log_dt, b_re, b_im, c_re, c_im, d_skip):
    f32 = F32
    hp = lax.Precision.HIGHEST
    dt = jnp.exp(log_dt.astype(f32))[:, None]
    lr, li = lam_re.astype(f32), lam_im.astype(f32)
    mag = jnp.exp(lr * dt)
    ar, ai = mag * jnp.cos(li * dt), mag * jnp.sin(li * dt)
    den = lr * lr + li * li
    nr, ni = ar - 1.0, ai
    fr, fi = (nr * lr + ni * li) / den, (ni * lr - nr * li) / den
    bbr = fr[..., None] * b_re - fi[..., None] * b_im
    bbi = fr[..., None] * b_im + fi[..., None] * b_re
    taus = jnp.arange(S5_T + 1, dtype=f32)[:, None, None]
    pmag = jnp.exp(taus * (lr * dt)[None])
    pr, pi = pmag * jnp.cos(taus * (li * dt)[None]), pmag * jnp.sin(taus * (li * dt)[None])
    lbr = pr[..., None] * bbr[None] - pi[..., None] * bbi[None]
    lbi = pr[..., None] * bbi[None] + pi[..., None] * bbr[None]
    cr, ci = c_re.astype(f32), c_im.astype(f32)
    kmat = (jnp.einsum('ghp,tgpk->tghk', cr, lbr, precision=hp)
            - jnp.einsum('ghp,tgpk->tghk', ci, lbi, precision=hp))
    ng, gw = S5_SG_GROUPS, S5_GROUP
    row_g = jnp.arange(128) // gw
    kt = jnp.swapaxes(kmat[:S5_T], 2, 3).reshape(S5_T, S5_SGS, 128, gw)
    bd = jnp.tile(kt, (1, 1, 1, ng)) * (row_g[:, None] == row_g[None, :])
    bd = jnp.swapaxes(bd, 0, 1)
    st_g = jnp.arange(S5_SG_STATE) // S5_STATE
    wmask = row_g[:, None] == st_g[None, :]

    def w_part(lb):
        x = lb[S5_T - 1::-1][:S5_T].reshape(S5_T, S5_SGS, ng, S5_STATE, gw)
        x = jnp.transpose(x, (1, 0, 2, 4, 3)).reshape(S5_SGS, S5_T, 128, S5_STATE)
        x = jnp.tile(x, (1, 1, 1, ng)) * wmask
        return x.reshape(S5_SGS, S5_LANES, S5_SG_STATE)

    wm = jnp.concatenate([w_part(lbr), w_part(lbi)], axis=2)
    crt, cit = jnp.swapaxes(cr, 1, 2)[:, :, None, :], jnp.swapaxes(ci, 1, 2)[:, :, None, :]
    pr1 = jnp.transpose(pr[1:], (1, 2, 0))[:, :, :, None]
    pi1 = jnp.transpose(pi[1:], (1, 2, 0))[:, :, :, None]
    vr = crt * pr1 - cit * pi1
    vi = crt * pi1 + cit * pr1

    def v_part(v):
        x = v.reshape(S5_SGS, S5_SG_STATE, S5_T, gw)
        x = jnp.tile(x, (1, 1, 1, ng)) * (st_g[:, None, None] == row_g[None, None, :])
        return x.reshape(S5_SGS, S5_SG_STATE, S5_LANES)

    vm = jnp.concatenate([v_part(vr), -v_part(vi)], axis=1)
    mt = bd
    lam_r = pr[S5_T].reshape(S5_SGS, 1, S5_SG_STATE)
    lam_i = pi[S5_T].reshape(S5_SGS, 1, S5_SG_STATE)
    dexp = jnp.tile(d_skip.astype(f32).reshape(S5_SGS, 1, 1, 128), (1, 1, S5_T, 1)).reshape(S5_SGS, 1, S5_LANES)
    return mt.astype(BF16), wm.astype(BF16), vm.astype(BF16), lam_r, lam_i, dexp


def _s5_kernel(u_ref, bd_ref, w_ref, v_ref, ar_ref, ai_ref, d_ref, o_ref,
               zr_ref, zi_ref, sr_ref, si_ref, cr_ref, ci_ref, mt_ref, *, rows):
    @pl.when(pl.program_id(2) == 0)
    def _():
        cr_ref[...] = jnp.zeros(cr_ref.shape, F32)
        ci_ref[...] = jnp.zeros(ci_ref.shape, F32)

    @pl.when(jnp.logical_and(pl.program_id(1) == 0, pl.program_id(2) == 0))
    def _():
        zero = jnp.zeros((128, 128), BF16)
        for s in range(S5_T):
            for t in range(S5_T):
                mt_ref[s * 128:(s + 1) * 128, t * 128:(t + 1) * 128] = bd_ref[t - s] if t >= s else zero

    u = jnp.concatenate([u_ref[pl.ds(t, rows, stride=S5_T), :] for t in range(S5_T)], axis=1)
    ub = u.astype(BF16)

    zz = jnp.dot(ub, w_ref[...], preferred_element_type=F32)
    zr_ref[...] = zz[:, :S5_SG_STATE]
    zi_ref[...] = zz[:, S5_SG_STATE:]

    ar, ai = ar_ref[...], ai_ref[...]

    def step(r, carry):
        sr, si = carry
        sr_ref[pl.ds(r, 1), :] = sr
        si_ref[pl.ds(r, 1), :] = si
        zr = zr_ref[pl.ds(r, 1), :]
        zi = zi_ref[pl.ds(r, 1), :]
        return ar * sr - ai * si + zr, ar * si + ai * sr + zi

    sr, si = lax.fori_loop(0, rows, step, (cr_ref[...], ci_ref[...]))
    cr_ref[...] = sr
    ci_ref[...] = si

    sin = jnp.concatenate([sr_ref[...], si_ref[...]], axis=1).astype(BF16)
    yoff = jnp.dot(sin, v_ref[...], preferred_element_type=F32)
    tile = 256
    for ct in range(S5_LANES // tile):
        cols = slice(ct * tile, (ct + 1) * tile)
        kdim = (ct + 1) * tile
        y = jnp.dot(ub[:, :kdim], mt_ref[:kdim, cols], preferred_element_type=F32)
        y = jax.nn.gelu(y + yoff[:, cols] + u[:, cols] * d_ref[:, cols])
        for t in range(ct * tile // 128, (ct + 1) * tile // 128):
            o_ref[pl.ds(t, rows, stride=S5_T), :] = y[:, t * 128 - ct * tile:(t + 1) * 128 - ct * tile]


def _s5(h, tables, bsz, seq, rows):
    mt, wm, vm, lar, lai, dexp = tables
    n = bsz * seq
    nch = seq // S5_T
    rows = min(rows, nch)
    nt = nch // rows
    uw, uidx = _packed_block('s5_u')
    ublk = uidx * (uw // 128)

    def per_sg(shape):
        return pl.BlockSpec((None,) + shape, lambda a, b, i: (a,) + (0,) * len(shape))

    kern = functools.partial(_s5_kernel, rows=rows)
    return pl.pallas_call(
        kern,
        grid=(S5_SGS, bsz, nt),
        in_specs=[pl.BlockSpec((rows * S5_T, 128), lambda a, b, i: (b * nt + i, ublk + a)),
                  per_sg((S5_T, 128, 128)), per_sg((S5_LANES, 2 * S5_SG_STATE)),
                  per_sg((2 * S5_SG_STATE, S5_LANES)), per_sg((1, S5_SG_STATE)), per_sg((1, S5_SG_STATE)),
                  per_sg((1, S5_LANES))],
        out_specs=pl.BlockSpec((rows * S5_T, 128), lambda a, b, i: (b * nt + i, a)),
        out_shape=jax.ShapeDtypeStruct((n, S5_WIDTH), F32),
        scratch_shapes=[pltpu.VMEM((rows, S5_SG_STATE), F32), pltpu.VMEM((rows, S5_SG_STATE), F32),
                        pltpu.VMEM((rows, S5_SG_STATE), F32), pltpu.VMEM((rows, S5_SG_STATE), F32),
                        pltpu.VMEM((1, S5_SG_STATE), F32), pltpu.VMEM((1, S5_SG_STATE), F32),
                        pltpu.VMEM((S5_LANES, S5_LANES), BF16)],
        compiler_params=_cparams(3),
    )(h, mt, wm, vm, lar, lai, dexp)


def _mem_kernel(q_ref, z_ref, mk_ref, mv_ref, o_ref):
    scale = HEAD_DIM ** -0.5
    mk = mk_ref[...].astype(BF16)
    mv = mv_ref[...].astype(BF16)
    outs = []
    for h in range(MEM_HEADS):
        sl = slice(h * HEAD_DIM, (h + 1) * HEAD_DIM)
        qh = (q_ref[:, sl] * scale).astype(BF16)
        lg = lax.dot_general(qh, mk[:, sl], NT_DIMS, preferred_element_type=F32)
        m = jnp.max(lg, axis=-1, keepdims=True)
        p = jnp.exp(lg - m)
        p = p * (1.0 / jnp.sum(p, axis=-1, keepdims=True))
        outs.append(jnp.dot(p.astype(BF16), mv[:, sl], preferred_element_type=F32))
    o_ref[...] = (jnp.concatenate(outs, axis=1) * _silu(z_ref[...])).astype(o_ref.dtype)


def _mem_attn(h, memkv, bsz, seq, t=512):
    n = bsz * seq
    t = min(t, seq)
    nt = seq // t
    qw, qidx = _packed_block('mem_q')
    zw, zidx = _packed_block('mem_z')
    return pl.pallas_call(
        _mem_kernel,
        grid=(bsz, nt),
        in_specs=[pl.BlockSpec((t, qw), lambda b, i: (b * nt + i, qidx)),
                  pl.BlockSpec((t, zw), lambda b, i: (b * nt + i, zidx)),
                  pl.BlockSpec((MEM_LEN, MEM_WIDTH), lambda b, i: (b, 0)),
                  pl.BlockSpec((MEM_LEN, MEM_WIDTH), lambda b, i: (b, 1))],
        out_specs=pl.BlockSpec((t, MEM_WIDTH), lambda b, i: (b * nt + i, 0)),
        out_shape=jax.ShapeDtypeStruct((n, MEM_WIDTH), BF16),
        compiler_params=_cparams(2),
    )(h, h, memkv, memkv)


def _merge_kernel(x_ref, s5_ref, s5z_ref, att_ref, ssd_ref, mem_ref,
                  wglu_ref, bglu_ref, wg0_ref, wg1_ref, wg2_ref, wg3_ref, bg_ref,
                  ws5_ref, watt_ref, wssd_ref, wmem_ref,
                  wout_ref, lng_ref, lnb_ref, o_ref, xb_ref, ys5_ref, acc_ref):
    nidx = pl.program_id(1)
    wg_ref = (wg0_ref, wg1_ref, wg2_ref, wg3_ref)

    @pl.when(nidx == 0)
    def _():
        xb_ref[...] = x_ref[...].astype(BF16)
        y = s5_ref[...]
        glu = y * _sigmoid(jnp.dot(y.astype(BF16), wglu_ref[...], preferred_element_type=F32) + bglu_ref[...])
        ys5_ref[...] = (glu * _silu(s5z_ref[...])).astype(BF16)
        acc_ref[...] = jnp.zeros(acc_ref.shape, F32)

    xb = xb_ref[...]
    branches = ((ys5_ref, ws5_ref), (att_ref, watt_ref), (ssd_ref, wssd_ref), (mem_ref, wmem_ref))
    merged = None
    for i, (y_ref, w_ref) in enumerate(branches):
        gate = _sigmoid(jnp.dot(xb, wg_ref[i][...], preferred_element_type=F32) + bg_ref[i])
        term = gate * jnp.dot(y_ref[...], w_ref[...], preferred_element_type=F32)
        merged = term if merged is None else merged + term
    acc_ref[...] += jnp.dot(merged.astype(BF16), wout_ref[...], preferred_element_type=F32)

    @pl.when(nidx == pl.num_programs(1) - 1)
    def _():
        r = DEEPNORM_ALPHA * x_ref[...] + acc_ref[...]
        mu = jnp.mean(r, axis=-1, keepdims=True)
        rc = r - mu
        var = jnp.mean(rc * rc, axis=-1, keepdims=True)
        o_ref[...] = rc * lax.rsqrt(var + LN_EPS) * lng_ref[...] + lnb_ref[...]


def _merge(x, h, ys5, yatt, yssd, ymem, wglu, bglu, wg4, bg4, ws5, watt, wssd, wmem, wout, lng, lnb,
           tm=512, tn=256):
    n = x.shape[0]
    tm = min(tm, n)
    zw, zidx = _packed_block('s5_z')
    row = lambda w: pl.BlockSpec((tm, w), lambda i, c: (i, 0))
    colw = lambda k: pl.BlockSpec((k, tn), lambda i, c: (0, c))
    const = lambda shape: pl.BlockSpec(shape, lambda i, c: (0,) * len(shape))
    gate_w = lambda br: pl.BlockSpec((D_MODEL, tn), lambda i, c: (0, br * (D_MODEL // tn) + c))
    return pl.pallas_call(
        _merge_kernel,
        grid=(n // tm, D_MODEL // tn),
        in_specs=[row(D_MODEL), row(S5_WIDTH), pl.BlockSpec((tm, zw), lambda i, c: (i, zidx)),
                  row(ATTN_WIDTH), row(SSD_WIDTH), row(MEM_WIDTH),
                  const((S5_WIDTH, S5_WIDTH)), const((1, S5_WIDTH)),
                  gate_w(0), gate_w(1), gate_w(2), gate_w(3),
                  pl.BlockSpec((N_BRANCH, 1, tn), lambda i, c: (0, 0, c)),
                  colw(S5_WIDTH), colw(ATTN_WIDTH), colw(SSD_WIDTH), colw(MEM_WIDTH),
                  pl.BlockSpec((tn, D_MODEL), lambda i, c: (c, 0)),
                  const((1, D_MODEL)), const((1, D_MODEL))],
        out_specs=pl.BlockSpec((tm, D_MODEL), lambda i, c: (i, 0)),
        out_shape=jax.ShapeDtypeStruct((n, D_MODEL), F32),
        scratch_shapes=[pltpu.VMEM((tm, D_MODEL), BF16), pltpu.VMEM((tm, S5_WIDTH), BF16),
                        pltpu.VMEM((tm, D_MODEL), F32)],
        compiler_params=_cparams(2),
    )(x, ys5, h, yatt, yssd, ymem, wglu, bglu, wg4, wg4, wg4, wg4, bg4, ws5, watt, wssd, wmem, wout, lng, lnb)


def _split_offsets():
    offs, off = {}, 0
    for name, width in SPLITS:
        offs[name] = (off, width)
        off += width
    return offs


def _pack_kernel(w_ref, p_ref, g_ref):
    src = _split_offsets()
    rows = w_ref.shape[0]

    def col(name):
        o, w = src[name]
        return w_ref[:, o:o + w]

    off = 0
    for name, width in PACKED:
        if name == 'idx_kw':
            piece = jnp.concatenate([col('idx_k'), col('idx_w'),
                                     jnp.zeros((rows, width - IDX_DIM - IDX_HEADS), F32)], axis=1)
        elif name == 'ssd_dt':
            piece = jnp.concatenate([col('ssd_dt'), jnp.zeros((rows, width - SSD_HEADS), F32)], axis=1)
        elif name == 'pad':
            piece = jnp.zeros((rows, width), F32)
        else:
            piece = col(name)
        p_ref[:, off:off + width] = piece.astype(BF16)
        off += width
    g_ref[...] = col('gates').astype(BF16)


def _split_w_in(w_in, layer, tr=128):
    _, d, win = w_in.shape
    gw = N_BRANCH * D_MODEL
    return pl.pallas_call(
        _pack_kernel,
        grid=(d // tr,),
        in_specs=[pl.BlockSpec((None, tr, win), lambda r: (layer, r, 0))],
        out_specs=[pl.BlockSpec((tr, PACKED_WIDTH), lambda r: (r, 0)),
                   pl.BlockSpec((tr, gw), lambda r: (r, 0))],
        out_shape=[jax.ShapeDtypeStruct((d, PACKED_WIDTH), BF16),
                   jax.ShapeDtypeStruct((d, gw), BF16)],
        compiler_params=_cparams(1),
    )(w_in)


def _rope_tables(positions):
    half = ROPE_DIM // 2
    inv = ROPE_THETA ** (-jnp.arange(half, dtype=F32) * 2.0 / ROPE_DIM)
    ang = positions.astype(F32).reshape(-1)[:, None] * inv
    cos, sin = jnp.cos(ang), jnp.sin(ang)
    n = ang.shape[0]
    ones = jnp.ones((n, HEAD_DIM - ROPE_DIM), F32)
    zeros = jnp.zeros((n, HEAD_DIM - ROPE_DIM), F32)
    zh = jnp.zeros((n, half), F32)
    c64 = jnp.concatenate([cos, cos, ones], axis=1)
    sp64 = jnp.concatenate([zh, sin, zeros], axis=1)
    sm64 = jnp.concatenate([-sin, zh, zeros], axis=1)
    dup = lambda t: jnp.concatenate([t, t], axis=1)
    return dup(c64), dup(sp64), dup(sm64)


def kernel(x, mem, positions, w_in, b_gate, s5_lam_re, s5_lam_im, s5_log_dt, s5_b_re, s5_b_im, s5_c_re, s5_c_im, s5_d, s5_w_glu, s5_b_glu, ssd_conv_w, ssd_conv_b, ssd_dt_bias, ssd_a_log, ssd_d, ssd_norm_g, mem_w_kv, w_br_s5, w_br_attn, w_br_ssd, w_br_mem, w_out, ln_g, ln_b):
    bsz, seq, d = x.shape
    n = bsz * seq
    depth = w_in.shape[0]
    qb = min(256, seq)
    nq = seq // qb
    ctab, sptab, smtab = _rope_tables(positions)
    xf = x.reshape(n, d).astype(F32)
    memf = mem.reshape(bsz * MEM_LEN, d).astype(F32)

    for i in range(depth):
        w_packed, w_gates = _split_w_in(w_in, i)
        h = _matmul(xf, w_packed, 1024, 512)
        memkv = _matmul(memf, mem_w_kv[i].astype(BF16), 512, 512)

        tables = _s5_tables(s5_lam_re[i], s5_lam_im[i], s5_log_dt[i], s5_b_re[i], s5_b_im[i],
                            s5_c_re[i], s5_c_im[i], s5_d[i])
        ys5 = _s5(h, tables, bsz, seq, 256)

        qp, iqp, kr, vt4, ki, wt = _prep(h, ctab, sptab, smtab, bsz, seq, qb)
        yatt = _dsa(h, qp, iqp, wt, ki, kr, vt4, bsz, seq, qb)

        yssd = _ssd(h, ssd_conv_w[i], ssd_conv_b[i], ssd_dt_bias[i], ssd_a_log[i], ssd_d[i],
                    ssd_norm_g[i], bsz, seq)

        ymem = _mem_attn(h, memkv, bsz, seq)

        xf = _merge(xf, h, ys5, yatt, yssd, ymem,
                    s5_w_glu[i].astype(BF16), s5_b_glu[i].astype(F32).reshape(1, S5_WIDTH),
                    w_gates, b_gate[i].astype(F32).reshape(N_BRANCH, 1, D_MODEL),
                    w_br_s5[i].astype(BF16), w_br_attn[i].astype(BF16), w_br_ssd[i].astype(BF16),
                    w_br_mem[i].astype(BF16), w_out[i].astype(BF16),
                    ln_g[i].astype(F32).reshape(1, D_MODEL), ln_b[i].astype(F32).reshape(1, D_MODEL))
    return xf.reshape(bsz, seq, d).astype(x.dtype)
```

```python
import functools
import math

import jax
import jax.numpy as jnp
from jax import lax
from jax.experimental import pallas as pl
from jax.experimental.pallas import tpu as pltpu

F32 = jnp.float32
BF16 = jnp.bfloat16

D_MODEL = 2048
S5_WIDTH = 512
S5_GROUP = 16
S5_GROUPS = 32
S5_STATE = 64
HEAD_DIM = 64
ATTN_HEADS = 8
ATTN_KV_HEADS = 2
ATTN_WIDTH = 512
IDX_HEADS = 4
IDX_DIM = 64
TOPK_MAX = 256
SSD_WIDTH = 1024
SSD_HEAD_DIM = 64
SSD_HEADS = 16
SSD_GROUPS = 2
SSD_STATE = 128
SSD_CONV = 4
SSD_CHUNK = 128
SSD_XBC = 1536
MEM_LEN = 256
MEM_HEADS = 4
MEM_WIDTH = 256
N_BRANCH = 4
ROPE_THETA = 500000.0
ROPE_DIM = 16
DEPTH = 2
DEEPNORM_ALPHA = (2 * DEPTH) ** 0.25
LN_EPS = 1e-5
RMS_EPS = 1e-5

SPLITS = (
    ('s5_u', 512), ('s5_z', 512),
    ('att_q', 512), ('att_k', 128), ('att_v', 128), ('att_z', 512),
    ('idx_q', 256), ('idx_k', 64), ('idx_w', 4),
    ('ssd_z', 1024), ('ssd_xbc', 1536), ('ssd_dt', 16),
    ('mem_q', 256), ('mem_z', 256),
    ('gates', 8192),
)

PACKED = (
    ('ssd_z', 1024), ('s5_u', 512), ('ssd_xbc', 1536), ('s5_z', 512), ('att_q', 512), ('att_z', 512),
    ('idx_q', 256), ('mem_q', 256), ('mem_z', 256), ('att_k', 128), ('att_v', 128),
    ('idx_kw', 128), ('ssd_dt', 128), ('pad', 256),
)
PACKED_WIDTH = sum(w for _, w in PACKED)


def _packed_block(name):
    off = 0
    for n, w in PACKED:
        if n == name:
            assert off % w == 0
            return w, off // w
        off += w
    raise KeyError(name)


VMEM_LIMIT = 56 * 1024 * 1024


def _cparams(n_axes, vmem=VMEM_LIMIT):
    return pltpu.CompilerParams(dimension_semantics=("arbitrary",) * n_axes, vmem_limit_bytes=vmem)


def _silu(x):
    return x * (1.0 / (1.0 + jnp.exp(-x)))


def _sigmoid(x):
    return 1.0 / (1.0 + jnp.exp(-x))


def _mm_kernel(x_ref, w_ref, o_ref, xb_ref):
    @pl.when(pl.program_id(1) == 0)
    def _():
        xb_ref[...] = x_ref[...].astype(BF16)

    o_ref[...] = jnp.dot(xb_ref[...], w_ref[...], preferred_element_type=F32).astype(o_ref.dtype)


def _matmul(x, w, tm, tn, out_dtype=F32):
    m, k = x.shape
    n = w.shape[1]
    tm = min(tm, m)
    tn = min(tn, n)
    assert m % tm == 0 and n % tn == 0
    return pl.pallas_call(
        _mm_kernel,
        grid=(m // tm, n // tn),
        in_specs=[pl.BlockSpec((tm, k), lambda i, j: (i, 0)),
                  pl.BlockSpec((k, tn), lambda i, j: (0, j))],
        out_specs=pl.BlockSpec((tm, tn), lambda i, j: (i, j)),
        out_shape=jax.ShapeDtypeStruct((m, n), out_dtype),
        scratch_shapes=[pltpu.VMEM((tm, k), BF16)],
        compiler_params=_cparams(2),
    )(x, w)


def _rope128(x, c, sp, sm):
    return x * c + pltpu.roll(x, 8, 1) * sp + pltpu.roll(x, 120, 1) * sm


def _prep_kernel(q_ref, iq_ref, k_ref, v_ref, kw_ref, c_ref, sp_ref, sm_ref,
                 qo_ref, iqo_ref, ko_ref, vo_ref, kio_ref, wo_ref):
    c, sp, sm = c_ref[...], sp_ref[...], sm_ref[...]
    t = q_ref.shape[0]
    zeros64 = jnp.zeros((t, 64), F32)
    scale = HEAD_DIM ** -0.5 * math.log2(math.e)
    for pair in range(ATTN_HEADS // 2):
        r = _rope128(q_ref[:, pair * 128:(pair + 1) * 128], c, sp, sm) * scale
        for sub in range(2):
            h = pair * 2 + sub
            part = r[:, sub * 64:(sub + 1) * 64]
            if h // (ATTN_HEADS // ATTN_KV_HEADS) == 0:
                full = jnp.concatenate([part, zeros64], axis=1)
            else:
                full = jnp.concatenate([zeros64, part], axis=1)
            qo_ref[h] = full.astype(BF16)
    for pair in range(IDX_HEADS // 2):
        r = _rope128(iq_ref[:, pair * 128:(pair + 1) * 128], c, sp, sm)
        for sub in range(2):
            iqo_ref[pair * 2 + sub] = r[:, sub * 64:(sub + 1) * 64].astype(BF16)
    ko_ref[...] = _rope128(k_ref[...], c, sp, sm).astype(BF16)
    vt = v_ref[...].T
    nchunk, _, _, qb = vo_ref.shape
    tail = jnp.where(lax.broadcasted_iota(jnp.int32, (V_ROWS - HEAD_DIM, qb), 0) == 0, 1.0, 0.0).astype(BF16)
    for ci in range(nchunk):
        for g in range(ATTN_KV_HEADS):
            vo_ref[ci, g, 0:HEAD_DIM, :] = vt[g * HEAD_DIM:(g + 1) * HEAD_DIM, ci * qb:(ci + 1) * qb].astype(BF16)
            vo_ref[ci, g, HEAD_DIM:V_ROWS, :] = tail
    kw = kw_ref[...]
    kio_ref[...] = _rope128(kw, c, sp, sm)[:, :64].astype(BF16)
    wo_ref[...] = kw.T[64:64 + IDX_HEADS, :] * ((IDX_HEADS ** -0.5) * (IDX_DIM ** -0.5))


def _prep(h, ctab, sptab, smtab, bsz, seq, qb, t=512):
    n = bsz * seq
    t = min(t, seq)
    nt = seq // t
    assert t % qb == 0

    def hspec(name):
        w, idx = _packed_block(name)
        return pl.BlockSpec((t, w), lambda b, i: (b * nt + i, idx))

    def rows(w):
        return pl.BlockSpec((t, w), lambda b, i: (b * nt + i, 0))

    return pl.pallas_call(
        _prep_kernel,
        grid=(bsz, nt),
        in_specs=[hspec('att_q'), hspec('idx_q'), hspec('att_k'), hspec('att_v'), hspec('idx_kw'),
                  rows(128), rows(128), rows(128)],
        out_specs=[pl.BlockSpec((ATTN_HEADS, t, 128), lambda b, i: (0, b * nt + i, 0)),
                   pl.BlockSpec((IDX_HEADS, t, 64), lambda b, i: (0, b * nt + i, 0)),
                   rows(128),
                   pl.BlockSpec((None, t // qb, ATTN_KV_HEADS, V_ROWS, qb), lambda b, i: (b, i, 0, 0, 0)),
                   rows(64),
                   pl.BlockSpec((None, IDX_HEADS, t), lambda b, i: (b, 0, i))],
        out_shape=[jax.ShapeDtypeStruct((ATTN_HEADS, n, 128), BF16),
                   jax.ShapeDtypeStruct((IDX_HEADS, n, 64), BF16),
                   jax.ShapeDtypeStruct((n, 128), BF16),
                   jax.ShapeDtypeStruct((bsz, seq // qb, ATTN_KV_HEADS, V_ROWS, qb), BF16),
                   jax.ShapeDtypeStruct((n, 64), BF16),
                   jax.ShapeDtypeStruct((bsz, IDX_HEADS, seq), F32)],
        compiler_params=_cparams(2),
    )(h, h, h, h, h, ctab, sptab, smtab)


BISECT_STEPS = 15
V_ROWS = 80
NT_DIMS = (((1,), (1,)), ((), ()))


def _dsa_kernel(q_ref, iq_ref, wt_ref, az_ref, ki_ref, k_ref, vt_ref, o_ref,
                s_ref, acc_ref, l_ref, lga_ref, lgb_ref, *, qb, topk):
    kb = qb
    j = pl.program_id(1)
    q0 = j * qb
    nk = j + 1
    neg = -jnp.inf
    qpos = q0 + lax.broadcasted_iota(jnp.int32, (1, qb), 1)
    row = lax.broadcasted_iota(jnp.int32, (kb, qb), 0)
    wt = wt_ref[...]

    def chunk(c):
        if isinstance(c, int):
            return pl.ds(c * kb, kb)
        return pl.ds(pl.multiple_of(c * kb, kb), kb)

    def p1(c, carry):
        mx, mn = carry
        kic = ki_ref[chunk(c), :]
        acc = jnp.zeros((kb, qb), F32)
        for h in range(IDX_HEADS):
            d = lax.dot_general(kic, iq_ref[h], NT_DIMS, preferred_element_type=F32)
            acc = acc + jnp.maximum(d, 0.0) * wt[h:h + 1, :]
        vis = (row + c * kb) <= qpos
        s_ref[chunk(c), :] = jnp.where(vis, acc, neg)
        mx = jnp.maximum(mx, jnp.max(jnp.where(vis, acc, neg), axis=0, keepdims=True))
        mn = jnp.minimum(mn, jnp.min(jnp.where(vis, acc, jnp.inf), axis=0, keepdims=True))
        return mx, mn

    mx, mn = lax.fori_loop(0, nk, p1, (jnp.full((1, qb), neg, F32), jnp.full((1, qb), jnp.inf, F32)))

    @pl.when(nk % 2 == 1)
    def _():
        s_ref[chunk(nk), :] = jnp.full((kb, qb), neg, F32)

    npair = (nk + 1) // 2

    sub = 64

    def sweep(fn, init):
        def body(i, a):
            base = pl.multiple_of(i * (2 * kb), 2 * kb)
            for r in range(0, 2 * kb, sub):
                a = fn(a, s_ref[pl.ds(base + r, sub), :])
            return a
        return lax.fori_loop(0, npair, body, init)

    def count(pred_fn):
        a = sweep(lambda a, s: a + jnp.where(pred_fn(s), 1.0, 0.0).reshape(sub // 8, 8, qb).sum(axis=0),
                  jnp.zeros((8, qb), F32))
        return jnp.sum(a, axis=0, keepdims=True)

    def max_below(hi):
        a = sweep(lambda a, s: jnp.maximum(a, jnp.where(s < hi, s, neg).reshape(sub // 8, 8, qb).max(axis=0)),
                  jnp.full((8, qb), neg, F32))
        return jnp.max(a, axis=0, keepdims=True)

    nvis = (qpos + 1).astype(F32)
    keff = jnp.minimum(nvis, float(topk))

    def bis_cond(carry):
        return jnp.logical_and(carry[0] > 0.0, carry[1] < BISECT_STEPS)

    def bis_body(carry):
        _, it, lo, hi, clo = carry
        top = jnp.minimum(hi, mx)
        mid = lo + 0.5 * (top - lo)
        cnt = count(lambda s: s >= mid)
        ok = cnt >= keff
        lo = jnp.where(ok, mid, lo)
        clo = jnp.where(ok, cnt, clo)
        hi = jnp.where(ok, hi, mid)
        return jnp.sum(jnp.where(clo == keff, 0.0, 1.0)), it + 1, lo, hi, clo

    _, _, lo, hi, clo = lax.while_loop(
        bis_cond, bis_body,
        (jnp.sum(jnp.where(nvis == keff, 0.0, 1.0)), jnp.int32(0), mn, jnp.full((1, qb), jnp.inf, F32), nvis))

    def fin_cond(carry):
        return carry[0] > 0.0

    def fin_body(carry):
        _, hi, thr, done = carry
        cand = max_below(hi)
        ok = count(lambda s: s >= cand) >= keff
        newly = jnp.logical_and(ok, done < 0.5)
        thr = jnp.where(newly, cand, thr)
        hi = jnp.where(jnp.logical_or(ok, done > 0.5), hi, cand)
        done = jnp.where(ok, 1.0, done)
        return jnp.sum(1.0 - done), hi, thr, done

    done0 = jnp.where(clo == keff, 1.0, 0.0)
    _, _, thr, _ = lax.while_loop(fin_cond, fin_body, (jnp.sum(1.0 - done0), hi, lo, done0))
    need = keff - count(lambda s: s > thr)

    l_ref[...] = jnp.zeros(l_ref.shape, F32)
    acc_ref[...] = jnp.zeros(acc_ref.shape, F32)
    tril = jnp.where(lax.broadcasted_iota(jnp.int32, (kb, kb), 1) <= lax.broadcasted_iota(jnp.int32, (kb, kb), 0),
                     1.0, 0.0).astype(BF16)

    def stage1(c, run, m_old, dst_ref):
        s = s_ref[chunk(c), :]
        eq = jnp.where(s == thr, 1.0, 0.0)
        rank = jnp.dot(tril, eq.astype(BF16), preferred_element_type=F32) + run
        tie_ok = jnp.where(s == thr, rank, jnp.inf) <= need
        bias = jnp.where(s > thr, 0.0, jnp.where(tie_ok, 0.0, neg))
        kc = k_ref[chunk(c), :]
        cms = []
        for h in range(ATTN_HEADS):
            lg = lax.dot_general(kc, q_ref[h], NT_DIMS, preferred_element_type=F32) + bias
            dst_ref[h] = lg
            cms.append(jnp.max(lg.reshape(kb // 8, 8, qb).max(axis=0), axis=0, keepdims=True))
        m_new = jnp.maximum(m_old, jnp.concatenate(cms, axis=0))
        m_safe = jnp.where(m_new == neg, 0.0, m_new)
        alpha = jnp.exp2(m_old - m_safe)
        return rank[kb - 1:kb, :], m_new, m_safe, alpha

    def stage2(c, m_safe, alpha, src_ref):
        lsum = []
        for h in range(ATTN_HEADS):
            g = h // (ATTN_HEADS // ATTN_KV_HEADS)
            p = jnp.exp2(src_ref[h] - m_safe[h:h + 1, :])
            pv = jnp.dot(vt_ref[c, g], p.astype(BF16), preferred_element_type=F32)
            lsum.append(pv[HEAD_DIM:HEAD_DIM + 1, :])
            acc_ref[h * 64:(h + 1) * 64, :] = alpha[h:h + 1, :] * acc_ref[h * 64:(h + 1) * 64, :] + pv[:HEAD_DIM, :]
        l_ref[...] = alpha * l_ref[...] + jnp.concatenate(lsum, axis=0)

    def step(c, carry, src_ref, dst_ref):
        run, m_old, m_safe_p, alpha_p = carry
        s = s_ref[chunk(c), :]
        eq = jnp.where(s == thr, 1.0, 0.0)
        rank = jnp.dot(tril, eq.astype(BF16), preferred_element_type=F32) + run
        tie_ok = jnp.where(s == thr, rank, jnp.inf) <= need
        bias = jnp.where(s > thr, 0.0, jnp.where(tie_ok, 0.0, neg))
        kc = k_ref[chunk(c), :]
        cms, lsum = [], []
        for h in range(ATTN_HEADS):
            g = h // (ATTN_HEADS // ATTN_KV_HEADS)
            lg = lax.dot_general(kc, q_ref[h], NT_DIMS, preferred_element_type=F32) + bias
            dst_ref[h] = lg
            cms.append(jnp.max(lg.reshape(kb // 8, 8, qb).max(axis=0), axis=0, keepdims=True))
            p = jnp.exp2(src_ref[h] - m_safe_p[h:h + 1, :])
            pv = jnp.dot(vt_ref[c - 1, g], p.astype(BF16), preferred_element_type=F32)
            lsum.append(pv[HEAD_DIM:HEAD_DIM + 1, :])
            acc_ref[h * 64:(h + 1) * 64, :] = (alpha_p[h:h + 1, :] * acc_ref[h * 64:(h + 1) * 64, :]
                                               + pv[:HEAD_DIM, :])
        l_ref[...] = alpha_p * l_ref[...] + jnp.concatenate(lsum, axis=0)
        m_new = jnp.maximum(m_old, jnp.concatenate(cms, axis=0))
        m_safe = jnp.where(m_new == neg, 0.0, m_new)
        alpha = jnp.exp2(m_old - m_safe)
        return rank[kb - 1:kb, :], m_new, m_safe, alpha

    def p3(i, carry):
        carry = step(2 * i + 1, carry, lga_ref, lgb_ref)
        return step(2 * i + 2, carry, lgb_ref, lga_ref)

    carry = stage1(0, jnp.zeros((1, qb), F32), jnp.full((ATTN_HEADS, qb), neg, F32), lga_ref)
    carry = lax.fori_loop(0, npair - 1, p3, carry)
    _, _, m_safe, alpha = step(2 * npair - 1, carry, lga_ref, lgb_ref)
    stage2(2 * npair - 1, m_safe, alpha, lgb_ref)

    for h in range(ATTN_HEADS):
        acc_ref[h * 64:(h + 1) * 64, :] = acc_ref[h * 64:(h + 1) * 64, :] * (1.0 / l_ref[h:h + 1, :])
    o_ref[...] = (acc_ref[...].T * _silu(az_ref[...])).astype(o_ref.dtype)


def _dsa(h, qp, iqp, wt, ki, kr, vt4, bsz, seq, qb):
    n = bsz * seq
    nq = seq // qb
    assert nq % 2 == 0
    topk = min(TOPK_MAX, seq // 4)
    zw, zidx = _packed_block('att_z')
    kern = functools.partial(_dsa_kernel, qb=qb, topk=topk)
    return pl.pallas_call(
        kern,
        grid=(bsz, nq),
        in_specs=[pl.BlockSpec((ATTN_HEADS, qb, 128), lambda b, j: (0, b * nq + j, 0)),
                  pl.BlockSpec((IDX_HEADS, qb, 64), lambda b, j: (0, b * nq + j, 0)),
                  pl.BlockSpec((None, IDX_HEADS, qb), lambda b, j: (b, 0, j)),
                  pl.BlockSpec((qb, zw), lambda b, j: (b * nq + j, zidx)),
                  pl.BlockSpec((seq, 64), lambda b, j: (b, 0)),
                  pl.BlockSpec((seq, 128), lambda b, j: (b, 0)),
                  pl.BlockSpec((None, nq, ATTN_KV_HEADS, V_ROWS, qb), lambda b, j: (b, 0, 0, 0, 0))],
        out_specs=pl.BlockSpec((qb, ATTN_WIDTH), lambda b, j: (b * nq + j, 0)),
        out_shape=jax.ShapeDtypeStruct((n, ATTN_WIDTH), BF16),
        scratch_shapes=[pltpu.VMEM((seq + qb, qb), F32),
                        pltpu.VMEM((ATTN_WIDTH, qb), F32),
                        pltpu.VMEM((ATTN_HEADS, qb), F32),
                        pltpu.VMEM((ATTN_HEADS, qb, qb), F32),
                        pltpu.VMEM((ATTN_HEADS, qb, qb), F32)],
        compiler_params=_cparams(2),
    )(qp, iqp, wt, h, ki, kr, vt4)


def _ssd_kernel(z_ref, xbc_ref, dt_ref, cw_ref, cb_ref, dtb_ref, a_ref, d_ref, ng_ref, o_ref,
                xc_ref, st_ref, y_ref):
    c = pl.program_id(1)
    L = SSD_CHUNK

    @pl.when(c == 0)
    def _():
        xc_ref[0:8, :] = jnp.zeros((8, SSD_XBC), F32)
        st_ref[...] = jnp.zeros(st_ref.shape, F32)

    xc_ref[8:8 + L, :] = xbc_ref[...]
    conv = cb_ref[...] + cw_ref[3:4, :] * xc_ref[8:8 + L, :]
    for k in range(1, SSD_CONV):
        conv = conv + cw_ref[3 - k:4 - k, :] * xc_ref[8 - k:8 - k + L, :]
    xc_ref[0:8, :] = xc_ref[L:L + 8, :]
    xbc = _silu(conv)

    dtr = dt_ref[...] + dtb_ref[...]
    dt = jnp.maximum(dtr, 0.0) + jnp.log1p(jnp.exp(-jnp.abs(dtr)))
    da = dt * a_ref[...]
    ri = lax.broadcasted_iota(jnp.int32, (L, L), 0)
    ci = lax.broadcasted_iota(jnp.int32, (L, L), 1)
    causal = ci <= ri
    tri = jnp.where(causal, 1.0, 0.0)
    acs = jnp.dot(tri, da, preferred_element_type=F32, precision=lax.Precision.HIGHEST)
    acs_t = acs.T
    e_acs = jnp.exp(acs)
    e_last = jnp.exp(acs[L - 1:L, :])
    dec = jnp.exp(acs[L - 1:L, :] - acs)

    hpg = SSD_HEADS // SSD_GROUPS
    for g in range(SSD_GROUPS):
        bm = xbc[:, SSD_WIDTH + g * SSD_STATE:SSD_WIDTH + (g + 1) * SSD_STATE]
        cm = xbc[:, SSD_WIDTH + (SSD_GROUPS + g) * SSD_STATE:SSD_WIDTH + (SSD_GROUPS + g + 1) * SSD_STATE]
        bmb = bm.astype(BF16)
        cmb = cm.astype(BF16)
        cb = lax.dot_general(cmb, bmb, NT_DIMS, preferred_element_type=F32)
        bmt = bm.T.astype(BF16)
        for jj in range(hpg):
            hd = g * hpg + jj
            xj = xbc[:, hd * SSD_HEAD_DIM:(hd + 1) * SSD_HEAD_DIM]
            xdt = xj * dt[:, hd:hd + 1]
            seg = acs[:, hd:hd + 1] - acs_t[hd:hd + 1, :]
            lm = jnp.exp(jnp.where(causal, seg, -jnp.inf))
            y = jnp.dot((cb * lm).astype(BF16), xdt.astype(BF16), preferred_element_type=F32)
            prev = st_ref[hd]
            y = y + jnp.dot(cmb, prev.astype(BF16), preferred_element_type=F32) * e_acs[:, hd:hd + 1]
            st = jnp.dot(bmt, (xdt * dec[:, hd:hd + 1]).astype(BF16), preferred_element_type=F32)
            st_ref[hd] = prev * e_last[:, hd:hd + 1] + st
            y_ref[:, hd * SSD_HEAD_DIM:(hd + 1) * SSD_HEAD_DIM] = y + xj * d_ref[:, hd * SSD_HEAD_DIM:(hd + 1) * SSD_HEAD_DIM]

    y = y_ref[...] * _silu(z_ref[...])
    gw = SSD_WIDTH // SSD_GROUPS
    parts = []
    for g in range(SSD_GROUPS):
        yg = y[:, g * gw:(g + 1) * gw]
        parts.append(yg * lax.rsqrt(jnp.mean(yg * yg, axis=-1, keepdims=True) + RMS_EPS))
    o_ref[...] = (jnp.concatenate(parts, axis=1) * ng_ref[...]).astype(o_ref.dtype)


def _ssd(h, conv_w, conv_b, dt_bias, a_log, d_skip, norm_g, bsz, seq):
    n = bsz * seq
    nc = seq // SSD_CHUNK
    L = SSD_CHUNK

    def hspec(name):
        w, idx = _packed_block(name)
        return pl.BlockSpec((L, w), lambda b, c: (b * nc + c, idx))

    def full(shape):
        return pl.BlockSpec(shape, lambda b, c: (0,) * len(shape))

    pad = 128 - SSD_HEADS
    dtb = jnp.pad(dt_bias.astype(F32), (0, pad)).reshape(1, 128)
    a = jnp.pad(-jnp.exp(a_log.astype(F32)), (0, pad)).reshape(1, 128)
    dexp = jnp.repeat(d_skip.astype(F32), SSD_HEAD_DIM).reshape(1, SSD_WIDTH)
    return pl.pallas_call(
        _ssd_kernel,
        grid=(bsz, nc),
        in_specs=[hspec('ssd_z'), hspec('ssd_xbc'), hspec('ssd_dt'),
                  full((SSD_CONV, SSD_XBC)), full((1, SSD_XBC)), full((1, 128)), full((1, 128)),
                  full((1, SSD_WIDTH)), full((1, SSD_WIDTH))],
        out_specs=pl.BlockSpec((L, SSD_WIDTH), lambda b, c: (b * nc + c, 0)),
        out_shape=jax.ShapeDtypeStruct((n, SSD_WIDTH), BF16),
        scratch_shapes=[pltpu.VMEM((L + 8, SSD_XBC), F32),
                        pltpu.VMEM((SSD_HEADS, SSD_STATE, SSD_HEAD_DIM), F32),
                        pltpu.VMEM((L, SSD_WIDTH), F32)],
        compiler_params=_cparams(2),
    )(h, h, h, conv_w.astype(F32), conv_b.astype(F32).reshape(1, SSD_XBC), dtb, a, dexp,
      norm_g.astype(F32).reshape(1, SSD_WIDTH))


S5_T = 16
S5_SG_GROUPS = 128 // S5_GROUP
S5_SGS = S5_GROUPS // S5_SG_GROUPS
S5_LANES = S5_T * 128
S5_SG_STATE = S5_SG_GROUPS * S5_STATE


def _s5_tables(lam_re, lam_im, log_dt, b_re, b_im, c_re, c_im, d_skip):
    f32 = F32
    hp = lax.Precision.HIGHEST
    dt = jnp.exp(log_dt.astype(f32))[:, None]
    lr, li = lam_re.astype(f32), lam_im.astype(f32)
    mag = jnp.exp(lr * dt)
    ar, ai = mag * jnp.cos(li * dt), mag * jnp.sin(li * dt)
    den = lr * lr + li * li
    nr, ni = ar - 1.0, ai
    fr, fi = (nr * lr + ni * li) / den, (ni * lr - nr * li) / den
    bbr = fr[..., None] * b_re - fi[..., None] * b_im
    bbi = fr[..., None] * b_im + fi[..., None] * b_re
    taus = jnp.arange(S5_T + 1, dtype=f32)[:, None, None]
    pmag = jnp.exp(taus * (lr * dt)[None])
    pr, pi = pmag * jnp.cos(taus * (li * dt)[None]), pmag * jnp.sin(taus * (li * dt)[None])
    lbr = pr[..., None] * bbr[None] - pi[..., None] * bbi[None]
    lbi = pr[..., None] * bbi[None] + pi[..., None] * bbr[None]
    cr, ci = c_re.astype(f32), c_im.astype(f32)
    kmat = (jnp.einsum('ghp,tgpk->tghk', cr, lbr, precision=hp)
            - jnp.einsum('ghp,tgpk->tghk', ci, lbi, precision=hp))
    ng, gw = S5_SG_GROUPS, S5_GROUP
    row_g = jnp.arange(128) // gw
    kt = jnp.swapaxes(kmat[:S5_T], 2, 3).reshape(S5_T, S5_SGS, 128, gw)
    bd = jnp.tile(kt, (1, 1, 1, ng)) * (row_g[:, None] == row_g[None, :])
    bd = jnp.swapaxes(bd, 0, 1)
    st_g = jnp.arange(S5_SG_STATE) // S5_STATE
    wmask = row_g[:, None] == st_g[None, :]

    def w_part(lb):
        x = lb[S5_T - 1::-1][:S5_T].reshape(S5_T, S5_SGS, ng, S5_STATE, gw)
        x = jnp.transpose(x, (1, 0, 2, 4, 3)).reshape(S5_SGS, S5_T, 128, S5_STATE)
        x = jnp.tile(x, (1, 1, 1, ng)) * wmask
        return x.reshape(S5_SGS, S5_LANES, S5_SG_STATE)

    wm = jnp.concatenate([w_part(lbr), w_part(lbi)], axis=2)
    crt, cit = jnp.swapaxes(cr, 1, 2)[:, :, None, :], jnp.swapaxes(ci, 1, 2)[:, :, None, :]
    pr1 = jnp.transpose(pr[1:], (1, 2, 0))[:, :, :, None]
    pi1 = jnp.transpose(pi[1:], (1, 2, 0))[:, :, :, None]
    vr = crt * pr1 - cit * pi1
    vi = crt * pi1 + cit * pr1

    def v_part(v):
        x = v.reshape(S5_SGS, S5_SG_STATE, S5_T, gw)
        x = jnp.tile(x, (1, 1, 1, ng)) * (st_g[:, None, None] == row_g[None, None, :])
        return x.reshape(S5_SGS, S5_SG_STATE, S5_LANES)

    vm = jnp.concatenate([v_part(vr), -v_part(vi)], axis=1)
    mt = bd
    lam_r = pr[S5_T].reshape(S5_SGS, 1, S5_SG_STATE)
    lam_i = pi[S5_T].reshape(S5_SGS, 1, S5_SG_STATE)
    dexp = jnp.tile(d_skip.astype(f32).reshape(S5_SGS, 1, 1, 128), (1, 1, S5_T, 1)).reshape(S5_SGS, 1, S5_LANES)
    return mt.astype(BF16), wm.astype(BF16), vm.astype(BF16), lam_r, lam_i, dexp


def _s5_kernel(u_ref, bd_ref, w_ref, v_ref, ar_ref, ai_ref, d_ref, o_ref,
               zr_ref, zi_ref, sr_ref, si_ref, cr_ref, ci_ref, mt_ref, *, rows):
    @pl.when(pl.program_id(2) == 0)
    def _():
        cr_ref[...] = jnp.zeros(cr_ref.shape, F32)
        ci_ref[...] = jnp.zeros(ci_ref.shape, F32)

    @pl.when(jnp.logical_and(pl.program_id(1) == 0, pl.program_id(2) == 0))
    def _():
        zero = jnp.zeros((128, 128), BF16)
        for s in range(S5_T):
            for t in range(S5_T):
                mt_ref[s * 128:(s + 1) * 128, t * 128:(t + 1) * 128] = bd_ref[t - s] if t >= s else zero

    u = jnp.concatenate([u_ref[pl.ds(t, rows, stride=S5_T), :] for t in range(S5_T)], axis=1)
    ub = u.astype(BF16)

    zz = jnp.dot(ub, w_ref[...], preferred_element_type=F32)
    zr_ref[...] = zz[:, :S5_SG_STATE]
    zi_ref[...] = zz[:, S5_SG_STATE:]

    ar, ai = ar_ref[...], ai_ref[...]

    def step(r, carry):
        sr, si = carry
        sr_ref[pl.ds(r, 1), :] = sr
        si_ref[pl.ds(r, 1), :] = si
        zr = zr_ref[pl.ds(r, 1), :]
        zi = zi_ref[pl.ds(r, 1), :]
        return ar * sr - ai * si + zr, ar * si + ai * sr + zi

    sr, si = lax.fori_loop(0, rows, step, (cr_ref[...], ci_ref[...]))
    cr_ref[...] = sr
    ci_ref[...] = si

    sin = jnp.concatenate([sr_ref[...], si_ref[...]], axis=1).astype(BF16)
    yoff = jnp.dot(sin, v_ref[...], preferred_element_type=F32)
    tile = 256
    for ct in range(S5_LANES // tile):
        cols = slice(ct * tile, (ct + 1) * tile)
        kdim = (ct + 1) * tile
        y = jnp.dot(ub[:, :kdim], mt_ref[:kdim, cols], preferred_element_type=F32)
        y = jax.nn.gelu(y + yoff[:, cols] + u[:, cols] * d_ref[:, cols])
        for t in range(ct * tile // 128, (ct + 1) * tile // 128):
            o_ref[pl.ds(t, rows, stride=S5_T), :] = y[:, t * 128 - ct * tile:(t + 1) * 128 - ct * tile]


def _s5(h, tables, bsz, seq, rows):
    mt, wm, vm, lar, lai, dexp = tables
    n = bsz * seq
    nch = seq // S5_T
    rows = min(rows, nch)
    nt = nch // rows
    uw, uidx = _packed_block('s5_u')
    ublk = uidx * (uw // 128)

    def per_sg(shape):
        return pl.BlockSpec((None,) + shape, lambda a, b, i: (a,) + (0,) * len(shape))

    kern = functools.partial(_s5_kernel, rows=rows)
    return pl.pallas_call(
        kern,
        grid=(S5_SGS, bsz, nt),
        in_specs=[pl.BlockSpec((rows * S5_T, 128), lambda a, b, i: (b * nt + i, ublk + a)),
                  per_sg((S5_T, 128, 128)), per_sg((S5_LANES, 2 * S5_SG_STATE)),
                  per_sg((2 * S5_SG_STATE, S5_LANES)), per_sg((1, S5_SG_STATE)), per_sg((1, S5_SG_STATE)),
                  per_sg((1, S5_LANES))],
        out_specs=pl.BlockSpec((rows * S5_T, 128), lambda a, b, i: (b * nt + i, a)),
        out_shape=jax.ShapeDtypeStruct((n, S5_WIDTH), F32),
        scratch_shapes=[pltpu.VMEM((rows, S5_SG_STATE), F32), pltpu.VMEM((rows, S5_SG_STATE), F32),
                        pltpu.VMEM((rows, S5_SG_STATE), F32), pltpu.VMEM((rows, S5_SG_STATE), F32),
                        pltpu.VMEM((1, S5_SG_STATE), F32), pltpu.VMEM((1, S5_SG_STATE), F32),
                        pltpu.VMEM((S5_LANES, S5_LANES), BF16)],
        compiler_params=_cparams(3),
    )(h, mt, wm, vm, lar, lai, dexp)


def _mem_kernel(q_ref, z_ref, mk_ref, mv_ref, o_ref):
    scale = HEAD_DIM ** -0.5
    mk = mk_ref[...].astype(BF16)
    mv = mv_ref[...].astype(BF16)
    outs = []
    for h in range(MEM_HEADS):
        sl = slice(h * HEAD_DIM, (h + 1) * HEAD_DIM)
        qh = (q_ref[:, sl] * scale).astype(BF16)
        lg = lax.dot_general(qh, mk[:, sl], NT_DIMS, preferred_element_type=F32)
        m = jnp.max(lg, axis=-1, keepdims=True)
        p = jnp.exp(lg - m)
        p = p * (1.0 / jnp.sum(p, axis=-1, keepdims=True))
        outs.append(jnp.dot(p.astype(BF16), mv[:, sl], preferred_element_type=F32))
    o_ref[...] = (jnp.concatenate(outs, axis=1) * _silu(z_ref[...])).astype(o_ref.dtype)


def _mem_attn(h, memkv, bsz, seq, t=512):
    n = bsz * seq
    t = min(t, seq)
    nt = seq // t
    qw, qidx = _packed_block('mem_q')
    zw, zidx = _packed_block('mem_z')
    return pl.pallas_call(
        _mem_kernel,
        grid=(bsz, nt),
        in_specs=[pl.BlockSpec((t, qw), lambda b, i: (b * nt + i, qidx)),
                  pl.BlockSpec((t, zw), lambda b, i: (b * nt + i, zidx)),
                  pl.BlockSpec((MEM_LEN, MEM_WIDTH), lambda b, i: (b, 0)),
                  pl.BlockSpec((MEM_LEN, MEM_WIDTH), lambda b, i: (b, 1))],
        out_specs=pl.BlockSpec((t, MEM_WIDTH), lambda b, i: (b * nt + i, 0)),
        out_shape=jax.ShapeDtypeStruct((n, MEM_WIDTH), BF16),
        compiler_params=_cparams(2),
    )(h, h, memkv, memkv)


def _merge_kernel(x_ref, s5_ref, s5z_ref, att_ref, ssd_ref, mem_ref,
                  wglu_ref, bglu_ref, wg0_ref, wg1_ref, wg2_ref, wg3_ref, bg_ref,
                  ws5_ref, watt_ref, wssd_ref, wmem_ref,
                  wout_ref, lng_ref, lnb_ref, o_ref, xb_ref, ys5_ref, acc_ref):
    nidx = pl.program_id(1)
    wg_ref = (wg0_ref, wg1_ref, wg2_ref, wg3_ref)

    @pl.when(nidx == 0)
    def _():
        xb_ref[...] = x_ref[...].astype(BF16)
        y = s5_ref[...]
        glu = y * _sigmoid(jnp.dot(y.astype(BF16), wglu_ref[...], preferred_element_type=F32) + bglu_ref[...])
        ys5_ref[...] = (glu * _silu(s5z_ref[...])).astype(BF16)
        acc_ref[...] = jnp.zeros(acc_ref.shape, F32)

    xb = xb_ref[...]
    branches = ((ys5_ref, ws5_ref), (att_ref, watt_ref), (ssd_ref, wssd_ref), (mem_ref, wmem_ref))
    merged = None
    for i, (y_ref, w_ref) in enumerate(branches):
        gate = _sigmoid(jnp.dot(xb, wg_ref[i][...], preferred_element_type=F32) + bg_ref[i])
        term = gate * jnp.dot(y_ref[...], w_ref[...], preferred_element_type=F32)
        merged = term if merged is None else merged + term
    acc_ref[...] += jnp.dot(merged.astype(BF16), wout_ref[...], preferred_element_type=F32)

    @pl.when(nidx == pl.num_programs(1) - 1)
    def _():
        r = DEEPNORM_ALPHA * x_ref[...] + acc_ref[...]
        mu = jnp.mean(r, axis=-1, keepdims=True)
        rc = r - mu
        var = jnp.mean(rc * rc, axis=-1, keepdims=True)
        o_ref[...] = rc * lax.rsqrt(var + LN_EPS) * lng_ref[...] + lnb_ref[...]


def _merge(x, h, ys5, yatt, yssd, ymem, wglu, bglu, wg4, bg4, ws5, watt, wssd, wmem, wout, lng, lnb,
           tm=512, tn=256):
    n = x.shape[0]
    tm = min(tm, n)
    zw, zidx = _packed_block('s5_z')
    row = lambda w: pl.BlockSpec((tm, w), lambda i, c: (i, 0))
    colw = lambda k: pl.BlockSpec((k, tn), lambda i, c: (0, c))
    const = lambda shape: pl.BlockSpec(shape, lambda i, c: (0,) * len(shape))
    gate_w = lambda br: pl.BlockSpec((D_MODEL, tn), lambda i, c: (0, br * (D_MODEL // tn) + c))
    return pl.pallas_call(
        _merge_kernel,
        grid=(n // tm, D_MODEL // tn),
        in_specs=[row(D_MODEL), row(S5_WIDTH), pl.BlockSpec((tm, zw), lambda i, c: (i, zidx)),
                  row(ATTN_WIDTH), row(SSD_WIDTH), row(MEM_WIDTH),
                  const((S5_WIDTH, S5_WIDTH)), const((1, S5_WIDTH)),
                  gate_w(0), gate_w(1), gate_w(2), gate_w(3),
                  pl.BlockSpec((N_BRANCH, 1, tn), lambda i, c: (0, 0, c)),
                  colw(S5_WIDTH), colw(ATTN_WIDTH), colw(SSD_WIDTH), colw(MEM_WIDTH),
                  pl.BlockSpec((tn, D_MODEL), lambda i, c: (c, 0)),
                  const((1, D_MODEL)), const((1, D_MODEL))],
        out_specs=pl.BlockSpec((tm, D_MODEL), lambda i, c: (i, 0)),
        out_shape=jax.ShapeDtypeStruct((n, D_MODEL), F32),
        scratch_shapes=[pltpu.VMEM((tm, D_MODEL), BF16), pltpu.VMEM((tm, S5_WIDTH), BF16),
                        pltpu.VMEM((tm, D_MODEL), F32)],
        compiler_params=_cparams(2),
    )(x, ys5, h, yatt, yssd, ymem, wglu, bglu, wg4, wg4, wg4, wg4, bg4, ws5, watt, wssd, wmem, wout, lng, lnb)


def _split_offsets():
    offs, off = {}, 0
    for name, width in SPLITS:
        offs[name] = (off, width)
        off += width
    return offs


def _pack_kernel(w_ref, p_ref, g_ref):
    src = _split_offsets()
    rows = w_ref.shape[0]

    def col(name):
        o, w = src[name]
        return w_ref[:, o:o + w]

    off = 0
    for name, width in PACKED:
        if name == 'idx_kw':
            piece = jnp.concatenate([col('idx_k'), col('idx_w'),
                                     jnp.zeros((rows, width - IDX_DIM - IDX_HEADS), F32)], axis=1)
        elif name == 'ssd_dt':
            piece = jnp.concatenate([col('ssd_dt'), jnp.zeros((rows, width - SSD_HEADS), F32)], axis=1)
        elif name == 'pad':
            piece = jnp.zeros((rows, width), F32)
        else:
            piece = col(name)
        p_ref[:, off:off + width] = piece.astype(BF16)
        off += width
    g_ref[...] = col('gates').astype(BF16)


def _split_w_in(w_in, layer, tr=128):
    _, d, win = w_in.shape
    gw = N_BRANCH * D_MODEL
    return pl.pallas_call(
        _pack_kernel,
        grid=(d // tr,),
        in_specs=[pl.BlockSpec((None, tr, win), lambda r: (layer, r, 0))],
        out_specs=[pl.BlockSpec((tr, PACKED_WIDTH), lambda r: (r, 0)),
                   pl.BlockSpec((tr, gw), lambda r: (r, 0))],
        out_shape=[jax.ShapeDtypeStruct((d, PACKED_WIDTH), BF16),
                   jax.ShapeDtypeStruct((d, gw), BF16)],
        compiler_params=_cparams(1),
    )(w_in)


def _rope_tables(positions):
    half = ROPE_DIM // 2
    inv = ROPE_THETA ** (-jnp.arange(half, dtype=F32) * 2.0 / ROPE_DIM)
    ang = positions.astype(F32).reshape(-1)[:, None] * inv
    cos, sin = jnp.cos(ang), jnp.sin(ang)
    n = ang.shape[0]
    ones = jnp.ones((n, HEAD_DIM - ROPE_DIM), F32)
    zeros = jnp.zeros((n, HEAD_DIM - ROPE_DIM), F32)
    zh = jnp.zeros((n, half), F32)
    c64 = jnp.concatenate([cos, cos, ones], axis=1)
    sp64 = jnp.concatenate([zh, sin, zeros], axis=1)
    sm64 = jnp.concatenate([-sin, zh, zeros], axis=1)
    dup = lambda t: jnp.concatenate([t, t], axis=1)
    return dup(c64), dup(sp64), dup(sm64)


def kernel(x, mem, positions, w_in, b_gate, s5_lam_re, s5_lam_im, s5_log_dt, s5_b_re, s5_b_im, s5_c_re, s5_c_im, s5_d, s5_w_glu, s5_b_glu, ssd_conv_w, ssd_conv_b, ssd_dt_bias, ssd_a_log, ssd_d, ssd_norm_g, mem_w_kv, w_br_s5, w_br_attn, w_br_ssd, w_br_mem, w_out, ln_g, ln_b):
    bsz, seq, d = x.shape
    n = bsz * seq
    depth = w_in.shape[0]
    qb = min(256, seq)
    nq = seq // qb
    ctab, sptab, smtab = _rope_tables(positions)
    xf = x.reshape(n, d).astype(F32)
    memf = mem.reshape(bsz * MEM_LEN, d).astype(F32)

    for i in range(depth):
        w_packed, w_gates = _split_w_in(w_in, i)
        h = _matmul(xf, w_packed, 1024, 512)
        memkv = _matmul(memf, mem_w_kv[i].astype(BF16), 512, 512)

        tables = _s5_tables(s5_lam_re[i], s5_lam_im[i], s5_log_dt[i], s5_b_re[i], s5_b_im[i],
                            s5_c_re[i], s5_c_im[i], s5_d[i])
        ys5 = _s5(h, tables, bsz, seq, 256)

        qp, iqp, kr, vt4, ki, wt = _prep(h, ctab, sptab, smtab, bsz, seq, qb)
        yatt = _dsa(h, qp, iqp, wt, ki, kr, vt4, bsz, seq, qb)

        yssd = _ssd(h, ssd_conv_w[i], ssd_conv_b[i], ssd_dt_bias[i], ssd_a_log[i], ssd_d[i],
                    ssd_norm_g[i], bsz, seq)

        ymem = _mem_attn(h, memkv, bsz, seq)

        xf = _merge(xf, h, ys5, yatt, yssd, ymem,
                    s5_w_glu[i].astype(BF16), s5_b_glu[i].astype(F32).reshape(1, S5_WIDTH),
                    w_gates, b_gate[i].astype(F32).reshape(N_BRANCH, 1, D_MODEL),
                    w_br_s5[i].astype(BF16), w_br_attn[i].astype(BF16), w_br_ssd[i].astype(BF16),
                    w_br_mem[i].astype(BF16), w_out[i].astype(BF16),
                    ln_g[i].astype(F32).reshape(1, D_MODEL), ln_b[i].astype(F32).reshape(1, D_MODEL))
    return xf.reshape(bsz, seq, d).astype(x.dtype)
```

```python
import functools
import math

import jax
import jax.numpy as jnp
from jax import lax
from jax.experimental import pallas as pl
from jax.experimental.pallas import tpu as pltpu

F32 = jnp.float32
BF16 = jnp.bfloat16

D_MODEL = 2048
S5_WIDTH = 512
S5_GROUP = 16
S5_GROUPS = 32
S5_STATE = 64
HEAD_DIM = 64
ATTN_HEADS = 8
ATTN_KV_HEADS = 2
ATTN_WIDTH = 512
IDX_HEADS = 4
IDX_DIM = 64
TOPK_MAX = 256
SSD_WIDTH = 1024
SSD_HEAD_DIM = 64
SSD_HEADS = 16
SSD_GROUPS = 2
SSD_STATE = 128
SSD_CONV = 4
SSD_CHUNK = 128
SSD_XBC = 1536
MEM_LEN = 256
MEM_HEADS = 4
MEM_WIDTH = 256
N_BRANCH = 4
ROPE_THETA = 500000.0
ROPE_DIM = 16
DEPTH = 2
DEEPNORM_ALPHA = (2 * DEPTH) ** 0.25
LN_EPS = 1e-5
RMS_EPS = 1e-5

SPLITS = (
    ('s5_u', 512), ('s5_z', 512),
    ('att_q', 512), ('att_k', 128), ('att_v', 128), ('att_z', 512),
    ('idx_q', 256), ('idx_k', 64), ('idx_w', 4),
    ('ssd_z', 1024), ('ssd_xbc', 1536), ('ssd_dt', 16),
    ('mem_q', 256), ('mem_z', 256),
    ('gates', 8192),
)

PACKED = (
    ('ssd_z', 1024), ('s5_u', 512), ('ssd_xbc', 1536), ('s5_z', 512), ('att_q', 512), ('att_z', 512),
    ('idx_q', 256), ('mem_q', 256), ('mem_z', 256), ('att_k', 128), ('att_v', 128),
    ('idx_kw', 128), ('ssd_dt', 128), ('pad', 256),
)
PACKED_WIDTH = sum(w for _, w in PACKED)


def _packed_block(name):
    off = 0
    for n, w in PACKED:
        if n == name:
            assert off % w == 0
            return w, off // w
        off += w
    raise KeyError(name)


VMEM_LIMIT = 56 * 1024 * 1024


def _cparams(n_axes, vmem=VMEM_LIMIT):
    return pltpu.CompilerParams(dimension_semantics=("arbitrary",) * n_axes, vmem_limit_bytes=vmem)


def _silu(x):
    return x * (1.0 / (1.0 + jnp.exp(-x)))


def _sigmoid(x):
    return 1.0 / (1.0 + jnp.exp(-x))


def _mm_kernel(x_ref, w_ref, o_ref, xb_ref):
    @pl.when(pl.program_id(1) == 0)
    def _():
        xb_ref[...] = x_ref[...].astype(BF16)

    o_ref[...] = jnp.dot(xb_ref[...], w_ref[...], preferred_element_type=F32).astype(o_ref.dtype)


def _matmul(x, w, tm, tn, out_dtype=F32):
    m, k = x.shape
    n = w.shape[1]
    tm = min(tm, m)
    tn = min(tn, n)
    assert m % tm == 0 and n % tn == 0
    return pl.pallas_call(
        _mm_kernel,
        grid=(m // tm, n // tn),
        in_specs=[pl.BlockSpec((tm, k), lambda i, j: (i, 0)),
                  pl.BlockSpec((k, tn), lambda i, j: (0, j))],
        out_specs=pl.BlockSpec((tm, tn), lambda i, j: (i, j)),
        out_shape=jax.ShapeDtypeStruct((m, n), out_dtype),
        scratch_shapes=[pltpu.VMEM((tm, k), BF16)],
        compiler_params=_cparams(2),
    )(x, w)


def _rope128(x, c, sp, sm):
    return x * c + pltpu.roll(x, 8, 1) * sp + pltpu.roll(x, 120, 1) * sm


def _prep_kernel(q_ref, iq_ref, k_ref, v_ref, kw_ref, c_ref, sp_ref, sm_ref,
                 qo_ref, iqo_ref, ko_ref, vo_ref, kio_ref, wo_ref):
    c, sp, sm = c_ref[...], sp_ref[...], sm_ref[...]
    t = q_ref.shape[0]
    zeros64 = jnp.zeros((t, 64), F32)
    scale = HEAD_DIM ** -0.5 * math.log2(math.e)
    for pair in range(ATTN_HEADS // 2):
        r = _rope128(q_ref[:, pair * 128:(pair + 1) * 128], c, sp, sm) * scale
        for sub in range(2):
            h = pair * 2 + sub
            part = r[:, sub * 64:(sub + 1) * 64]
            if h // (ATTN_HEADS // ATTN_KV_HEADS) == 0:
                full = jnp.concatenate([part, zeros64], axis=1)
            else:
                full = jnp.concatenate([zeros64, part], axis=1)
            qo_ref[h] = full.astype(BF16)
    for pair in range(IDX_HEADS // 2):
        r = _rope128(iq_ref[:, pair * 128:(pair + 1) * 128], c, sp, sm)
        for sub in range(2):
            iqo_ref[pair * 2 + sub] = r[:, sub * 64:(sub + 1) * 64].astype(BF16)
    ko_ref[...] = _rope128(k_ref[...], c, sp, sm).astype(BF16)
    vt = v_ref[...].T
    nchunk, _, _, qb = vo_ref.shape
    tail = jnp.where(lax.broadcasted_iota(jnp.int32, (V_ROWS - HEAD_DIM, qb), 0) == 0, 1.0, 0.0).astype(BF16)
    for ci in range(nchunk):
        for g in range(ATTN_KV_HEADS):
            vo_ref[ci, g, 0:HEAD_DIM, :] = vt[g * HEAD_DIM:(g + 1) * HEAD_DIM, ci * qb:(ci + 1) * qb].astype(BF16)
            vo_ref[ci, g, HEAD_DIM:V_ROWS, :] = tail
    kw = kw_ref[...]
    kio_ref[...] = _rope128(kw, c, sp, sm)[:, :64].astype(BF16)
    wo_ref[...] = kw.T[64:64 + IDX_HEADS, :] * ((IDX_HEADS ** -0.5) * (IDX_DIM ** -0.5))


def _prep(h, ctab, sptab, smtab, bsz, seq, qb, t=512):
    n = bsz * seq
    t = min(t, seq)
    nt = seq // t
    assert t % qb == 0

    def hspec(name):
        w, idx = _packed_block(name)
        return pl.BlockSpec((t, w), lambda b, i: (b * nt + i, idx))

    def rows(w):
        return pl.BlockSpec((t, w), lambda b, i: (b * nt + i, 0))

    return pl.pallas_call(
        _prep_kernel,
        grid=(bsz, nt),
        in_specs=[hspec('att_q'), hspec('idx_q'), hspec('att_k'), hspec('att_v'), hspec('idx_kw'),
                  rows(128), rows(128), rows(128)],
        out_specs=[pl.BlockSpec((ATTN_HEADS, t, 128), lambda b, i: (0, b * nt + i, 0)),
                   pl.BlockSpec((IDX_HEADS, t, 64), lambda b, i: (0, b * nt + i, 0)),
                   rows(128),
                   pl.BlockSpec((None, t // qb, ATTN_KV_HEADS, V_ROWS, qb), lambda b, i: (b, i, 0, 0, 0)),
                   rows(64),
                   pl.BlockSpec((None, IDX_HEADS, t), lambda b, i: (b, 0, i))],
        out_shape=[jax.ShapeDtypeStruct((ATTN_HEADS, n, 128), BF16),
                   jax.ShapeDtypeStruct((IDX_HEADS, n, 64), BF16),
                   jax.ShapeDtypeStruct((n, 128), BF16),
                   jax.ShapeDtypeStruct((bsz, seq // qb, ATTN_KV_HEADS, V_ROWS, qb), BF16),
                   jax.ShapeDtypeStruct((n, 64), BF16),
                   jax.ShapeDtypeStruct((bsz, IDX_HEADS, seq), F32)],
        compiler_params=_cparams(2),
    )(h, h, h, h, h, ctab, sptab, smtab)


BISECT_STEPS = 15
V_ROWS = 80
NT_DIMS = (((1,), (1,)), ((), ()))


def _dsa_kernel(q_ref, iq_ref, wt_ref, az_ref, ki_ref, k_ref, vt_ref, o_ref,
                s_ref, acc_ref, l_ref, lga_ref, lgb_ref, *, qb, topk):
    kb = qb
    j = pl.program_id(1)
    q0 = j * qb
    nk = j + 1
    neg = -jnp.inf
    qpos = q0 + lax.broadcasted_iota(jnp.int32, (1, qb), 1)
    row = lax.broadcasted_iota(jnp.int32, (kb, qb), 0)
    wt = wt_ref[...]

    def chunk(c):
        if isinstance(c, int):
            return pl.ds(c * kb, kb)
        return pl.ds(pl.multiple_of(c * kb, kb), kb)

    def p1(c, carry):
        mx, mn = carry
        kic = ki_ref[chunk(c), :]
        acc = jnp.zeros((kb, qb), F32)
        for h in range(IDX_HEADS):
            d = lax.dot_general(kic, iq_ref[h], NT_DIMS, preferred_element_type=F32)
            acc = acc + jnp.maximum(d, 0.0) * wt[h:h + 1, :]
        vis = (row + c * kb) <= qpos
        s_ref[chunk(c), :] = jnp.where(vis, acc, neg)
        mx = jnp.maximum(mx, jnp.max(jnp.where(vis, acc, neg), axis=0, keepdims=True))
        mn = jnp.minimum(mn, jnp.min(jnp.where(vis, acc, jnp.inf), axis=0, keepdims=True))
        return mx, mn

    mx, mn = lax.fori_loop(0, nk, p1, (jnp.full((1, qb), neg, F32), jnp.full((1, qb), jnp.inf, F32)))

    @pl.when(nk % 2 == 1)
    def _():
        s_ref[chunk(nk), :] = jnp.full((kb, qb), neg, F32)

    npair = (nk + 1) // 2

    sub = 64

    def sweep(fn, init):
        def body(i, a):
            base = pl.multiple_of(i * (2 * kb), 2 * kb)
            for r in range(0, 2 * kb, sub):
                a = fn(a, s_ref[pl.ds(base + r, sub), :])
            return a
        return lax.fori_loop(0, npair, body, init)

    def count(pred_fn):
        a = sweep(lambda a, s: a + jnp.where(pred_fn(s), 1.0, 0.0).reshape(sub // 8, 8, qb).sum(axis=0),
                  jnp.zeros((8, qb), F32))
        return jnp.sum(a, axis=0, keepdims=True)

    def max_below(hi):
        a = sweep(lambda a, s: jnp.maximum(a, jnp.where(s < hi, s, neg).reshape(sub // 8, 8, qb).max(axis=0)),
                  jnp.full((8, qb), neg, F32))
        return jnp.max(a, axis=0, keepdims=True)

    nvis = (qpos + 1).astype(F32)
    keff = jnp.minimum(nvis, float(topk))

    def bis_cond(carry):
        return jnp.logical_and(carry[0] > 0.0, carry[1] < BISECT_STEPS)

    def bis_body(carry):
        _, it, lo, hi, clo, chi = carry
        top = jnp.minimum(hi, mx)
        mid = lo + 0.5 * (top - lo)
        cnt = count(lambda s: s >= mid)
        ok = cnt >= keff
        lo = jnp.where(ok, mid, lo)
        clo = jnp.where(ok, cnt, clo)
        hi = jnp.where(ok, hi, mid)
        chi = jnp.where(ok, chi, cnt)
        return jnp.sum(jnp.where(clo == keff, 0.0, 1.0)), it + 1, lo, hi, clo, chi

    _, _, lo, hi, clo, chi = lax.while_loop(
        bis_cond, bis_body,
        (jnp.sum(jnp.where(nvis == keff, 0.0, 1.0)), jnp.int32(0), mn, jnp.full((1, qb), jnp.inf, F32), nvis,
         jnp.zeros((1, qb), F32)))

    def fin_cond(carry):
        return carry[0] > 0.0

    def fin_body(carry):
        _, hi, chi, thr, done = carry
        cand = max_below(hi)
        cge = count(lambda s: s >= cand)
        ok = cge >= keff
        newly = jnp.logical_and(ok, done < 0.5)
        thr = jnp.where(newly, cand, thr)
        keep = jnp.logical_or(ok, done > 0.5)
        hi = jnp.where(keep, hi, cand)
        chi = jnp.where(keep, chi, cge)
        done = jnp.where(ok, 1.0, done)
        return jnp.sum(1.0 - done), hi, chi, thr, done

    done0 = jnp.where(clo == keff, 1.0, 0.0)
    _, _, chi, thr, _ = lax.while_loop(fin_cond, fin_body, (jnp.sum(1.0 - done0), hi, chi, lo, done0))
    need = jnp.where(done0 > 0.5, float(s_ref.shape[0]), keff - chi)

    l_ref[...] = jnp.zeros(l_ref.shape, F32)
    acc_ref[...] = jnp.zeros(acc_ref.shape, F32)
    tril = jnp.where(lax.broadcasted_iota(jnp.int32, (kb, kb), 1) <= lax.broadcasted_iota(jnp.int32, (kb, kb), 0),
                     1.0, 0.0).astype(BF16)

    def stage1(c, run, m_old, dst_ref):
        s = s_ref[chunk(c), :]
        eq = jnp.where(s == thr, 1.0, 0.0)
        rank = jnp.dot(tril, eq.astype(BF16), preferred_element_type=F32) + run
        tie_ok = jnp.where(s == thr, rank, jnp.inf) <= need
        bias = jnp.where(s > thr, 0.0, jnp.where(tie_ok, 0.0, neg))
        kc = k_ref[chunk(c), :]
        cms = []
        for h in range(ATTN_HEADS):
            lg = lax.dot_general(kc, q_ref[h], NT_DIMS, preferred_element_type=F32) + bias
            dst_ref[h] = lg
            cms.append(jnp.max(lg.reshape(kb // 8, 8, qb).max(axis=0), axis=0, keepdims=True))
        m_new = jnp.maximum(m_old, jnp.concatenate(cms, axis=0))
        m_safe = jnp.where(m_new == neg, 0.0, m_new)
        alpha = jnp.exp2(m_old - m_safe)
        return rank[kb - 1:kb, :], m_new, m_safe, alpha

    def stage2(c, m_safe, alpha, src_ref):
        lsum = []
        for h in range(ATTN_HEADS):
            g = h // (ATTN_HEADS // ATTN_KV_HEADS)
            p = jnp.exp2(src_ref[h] - m_safe[h:h + 1, :])
            pv = jnp.dot(vt_ref[c, g], p.astype(BF16), preferred_element_type=F32)
            lsum.append(pv[HEAD_DIM:HEAD_DIM + 1, :])
            acc_ref[h * 64:(h + 1) * 64, :] = alpha[h:h + 1, :] * acc_ref[h * 64:(h + 1) * 64, :] + pv[:HEAD_DIM, :]
        l_ref[...] = alpha * l_ref[...] + jnp.concatenate(lsum, axis=0)

    def step(c, carry, src_ref, dst_ref):
        run, m_old, m_safe_p, alpha_p = carry
        s = s_ref[chunk(c), :]
        eq = jnp.where(s == thr, 1.0, 0.0)
        rank = jnp.dot(tril, eq.astype(BF16), preferred_element_type=F32) + run
        tie_ok = jnp.where(s == thr, rank, jnp.inf) <= need
        bias = jnp.where(s > thr, 0.0, jnp.where(tie_ok, 0.0, neg))
        kc = k_ref[chunk(c), :]
        cms, lsum = [], []
        for h in range(ATTN_HEADS):
            g = h // (ATTN_HEADS // ATTN_KV_HEADS)
            lg = lax.dot_general(kc, q_ref[h], NT_DIMS, preferred_element_type=F32) + bias
            dst_ref[h] = lg
            cms.append(jnp.max(lg.reshape(kb // 8, 8, qb).max(axis=0), axis=0, keepdims=True))
            p = jnp.exp2(src_ref[h] - m_safe_p[h:h + 1, :])
            pv = jnp.dot(vt_ref[c - 1, g], p.astype(BF16), preferred_element_type=F32)
            lsum.append(pv[HEAD_DIM:HEAD_DIM + 1, :])
            acc_ref[h * 64:(h + 1) * 64, :] = (alpha_p[h:h + 1, :] * acc_ref[h * 64:(h + 1) * 64, :]
                                               + pv[:HEAD_DIM, :])
        l_ref[...] = alpha_p * l_ref[...] + jnp.concatenate(lsum, axis=0)
        m_new = jnp.maximum(m_old, jnp.concatenate(cms, axis=0))
        m_safe = jnp.where(m_new == neg, 0.0, m_new)
        alpha = jnp.exp2(m_old - m_safe)
        return rank[kb - 1:kb, :], m_new, m_safe, alpha

    def p3(i, carry):
        carry = step(2 * i + 1, carry, lga_ref, lgb_ref)
        return step(2 * i + 2, carry, lgb_ref, lga_ref)

    carry = stage1(0, jnp.zeros((1, qb), F32), jnp.full((ATTN_HEADS, qb), neg, F32), lga_ref)
    carry = lax.fori_loop(0, npair - 1, p3, carry)
    _, _, m_safe, alpha = step(2 * npair - 1, carry, lga_ref, lgb_ref)
    stage2(2 * npair - 1, m_safe, alpha, lgb_ref)

    for h in range(ATTN_HEADS):
        acc_ref[h * 64:(h + 1) * 64, :] = acc_ref[h * 64:(h + 1) * 64, :] * (1.0 / l_ref[h:h + 1, :])
    o_ref[...] = (acc_ref[...].T * _silu(az_ref[...])).astype(o_ref.dtype)


def _dsa(h, qp, iqp, wt, ki, kr, vt4, bsz, seq, qb):
    n = bsz * seq
    nq = seq // qb
    assert nq % 2 == 0
    topk = min(TOPK_MAX, seq // 4)
    zw, zidx = _packed_block('att_z')
    kern = functools.partial(_dsa_kernel, qb=qb, topk=topk)
    return pl.pallas_call(
        kern,
        grid=(bsz, nq),
        in_specs=[pl.BlockSpec((ATTN_HEADS, qb, 128), lambda b, j: (0, b * nq + j, 0)),
                  pl.BlockSpec((IDX_HEADS, qb, 64), lambda b, j: (0, b * nq + j, 0)),
                  pl.BlockSpec((None, IDX_HEADS, qb), lambda b, j: (b, 0, j)),
                  pl.BlockSpec((qb, zw), lambda b, j: (b * nq + j, zidx)),
                  pl.BlockSpec((seq, 64), lambda b, j: (b, 0)),
                  pl.BlockSpec((seq, 128), lambda b, j: (b, 0)),
                  pl.BlockSpec((None, nq, ATTN_KV_HEADS, V_ROWS, qb), lambda b, j: (b, 0, 0, 0, 0))],
        out_specs=pl.BlockSpec((qb, ATTN_WIDTH), lambda b, j: (b * nq + j, 0)),
        out_shape=jax.ShapeDtypeStruct((n, ATTN_WIDTH), BF16),
        scratch_shapes=[pltpu.VMEM((seq + qb, qb), F32),
                        pltpu.VMEM((ATTN_WIDTH, qb), F32),
                        pltpu.VMEM((ATTN_HEADS, qb), F32),
                        pltpu.VMEM((ATTN_HEADS, qb, qb), F32),
                        pltpu.VMEM((ATTN_HEADS, qb, qb), F32)],
        compiler_params=_cparams(2),
    )(qp, iqp, wt, h, ki, kr, vt4)


def _ssd_kernel(z_ref, xbc_ref, dt_ref, cw_ref, cb_ref, dtb_ref, a_ref, d_ref, ng_ref, o_ref,
                xc_ref, st_ref, y_ref):
    c = pl.program_id(1)
    L = SSD_CHUNK

    @pl.when(c == 0)
    def _():
        xc_ref[0:8, :] = jnp.zeros((8, SSD_XBC), F32)
        st_ref[...] = jnp.zeros(st_ref.shape, F32)

    xc_ref[8:8 + L, :] = xbc_ref[...]
    conv = cb_ref[...] + cw_ref[3:4, :] * xc_ref[8:8 + L, :]
    for k in range(1, SSD_CONV):
        conv = conv + cw_ref[3 - k:4 - k, :] * xc_ref[8 - k:8 - k + L, :]
    xc_ref[0:8, :] = xc_ref[L:L + 8, :]
    xbc = _silu(conv)

    dtr = dt_ref[...] + dtb_ref[...]
    dt = jnp.maximum(dtr, 0.0) + jnp.log1p(jnp.exp(-jnp.abs(dtr)))
    da = dt * a_ref[...]
    ri = lax.broadcasted_iota(jnp.int32, (L, L), 0)
    ci = lax.broadcasted_iota(jnp.int32, (L, L), 1)
    causal = ci <= ri
    tri = jnp.where(causal, 1.0, 0.0)
    acs = jnp.dot(tri, da, preferred_element_type=F32, precision=lax.Precision.HIGHEST)
    acs_t = acs.T
    e_acs = jnp.exp(acs)
    e_last = jnp.exp(acs[L - 1:L, :])
    dec = jnp.exp(acs[L - 1:L, :] - acs)

    hpg = SSD_HEADS // SSD_GROUPS
    for g in range(SSD_GROUPS):
        bm = xbc[:, SSD_WIDTH + g * SSD_STATE:SSD_WIDTH + (g + 1) * SSD_STATE]
        cm = xbc[:, SSD_WIDTH + (SSD_GROUPS + g) * SSD_STATE:SSD_WIDTH + (SSD_GROUPS + g + 1) * SSD_STATE]
        bmb = bm.astype(BF16)
        cmb = cm.astype(BF16)
        cb = lax.dot_general(cmb, bmb, NT_DIMS, preferred_element_type=F32)
        bmt = bm.T.astype(BF16)
        for jj in range(hpg):
            hd = g * hpg + jj
            xj = xbc[:, hd * SSD_HEAD_DIM:(hd + 1) * SSD_HEAD_DIM]
            xdt = xj * dt[:, hd:hd + 1]
            seg = acs[:, hd:hd + 1] - acs_t[hd:hd + 1, :]
            lm = jnp.exp(jnp.where(causal, seg, -jnp.inf))
            y = jnp.dot((cb * lm).astype(BF16), xdt.astype(BF16), preferred_element_type=F32)
            prev = st_ref[hd]
            y = y + jnp.dot(cmb, prev.astype(BF16), preferred_element_type=F32) * e_acs[:, hd:hd + 1]
            st = jnp.dot(bmt, (xdt * dec[:, hd:hd + 1]).astype(BF16), preferred_element_type=F32)
            st_ref[hd] = prev * e_last[:, hd:hd + 1] + st
            y_ref[:, hd * SSD_HEAD_DIM:(hd + 1) * SSD_HEAD_DIM] = y + xj * d_ref[:, hd * SSD_HEAD_DIM:(hd + 1) * SSD_HEAD_DIM]

    y = y_ref[...] * _silu(z_ref[...])
    gw = SSD_WIDTH // SSD_GROUPS
    parts = []
    for g in range(SSD_GROUPS):
        yg = y[:, g * gw:(g + 1) * gw]
        parts.append(yg * lax.rsqrt(jnp.mean(yg * yg, axis=-1, keepdims=True) + RMS_EPS))
    o_ref[...] = (jnp.concatenate(parts, axis=1) * ng_ref[...]).astype(o_ref.dtype)


def _ssd(h, conv_w, conv_b, dt_bias, a_log, d_skip, norm_g, bsz, seq):
    n = bsz * seq
    nc = seq // SSD_CHUNK
    L = SSD_CHUNK

    def hspec(name):
        w, idx = _packed_block(name)
        return pl.BlockSpec((L, w), lambda b, c: (b * nc + c, idx))

    def full(shape):
        return pl.BlockSpec(shape, lambda b, c: (0,) * len(shape))

    pad = 128 - SSD_HEADS
    dtb = jnp.pad(dt_bias.astype(F32), (0, pad)).reshape(1, 128)
    a = jnp.pad(-jnp.exp(a_log.astype(F32)), (0, pad)).reshape(1, 128)
    dexp = jnp.repeat(d_skip.astype(F32), SSD_HEAD_DIM).reshape(1, SSD_WIDTH)
    return pl.pallas_call(
        _ssd_kernel,
        grid=(bsz, nc),
        in_specs=[hspec('ssd_z'), hspec('ssd_xbc'), hspec('ssd_dt'),
                  full((SSD_CONV, SSD_XBC)), full((1, SSD_XBC)), full((1, 128)), full((1, 128)),
                  full((1, SSD_WIDTH)), full((1, SSD_WIDTH))],
        out_specs=pl.BlockSpec((L, SSD_WIDTH), lambda b, c: (b * nc + c, 0)),
        out_shape=jax.ShapeDtypeStruct((n, SSD_WIDTH), BF16),
        scratch_shapes=[pltpu.VMEM((L + 8, SSD_XBC), F32),
                        pltpu.VMEM((SSD_HEADS, SSD_STATE, SSD_HEAD_DIM), F32),
                        pltpu.VMEM((L, SSD_WIDTH), F32)],
        compiler_params=_cparams(2),
    )(h, h, h, conv_w.astype(F32), conv_b.astype(F32).reshape(1, SSD_XBC), dtb, a, dexp,
      norm_g.astype(F32).reshape(1, SSD_WIDTH))


S5_T = 16
S5_SG_GROUPS = 128 // S5_GROUP
S5_SGS = S5_GROUPS // S5_SG_GROUPS
S5_LANES = S5_T * 128
S5_SG_STATE = S5_SG_GROUPS * S5_STATE


def _s5_tables(lam_re, lam_im, log_dt, b_re, b_im, c_re, c_im, d_skip):
    f32 = F32
    hp = lax.Precision.HIGHEST
    dt = jnp.exp(log_dt.astype(f32))[:, None]
    lr, li = lam_re.astype(f32), lam_im.astype(f32)
    mag = jnp.exp(lr * dt)
    ar, ai = mag * jnp.cos(li * dt), mag * jnp.sin(li * dt)
    den = lr * lr + li * li
    nr, ni = ar - 1.0, ai
    fr, fi = (nr * lr + ni * li) / den, (ni * lr - nr * li) / den
    bbr = fr[..., None] * b_re - fi[..., None] * b_im
    bbi = fr[..., None] * b_im + fi[..., None] * b_re
    taus = jnp.arange(S5_T + 1, dtype=f32)[:, None, None]
    pmag = jnp.exp(taus * (lr * dt)[None])
    pr, pi = pmag * jnp.cos(taus * (li * dt)[None]), pmag * jnp.sin(taus * (li * dt)[None])
    lbr = pr[..., None] * bbr[None] - pi[..., None] * bbi[None]
    lbi = pr[..., None] * bbi[None] + pi[..., None] * bbr[None]
    cr, ci = c_re.astype(f32), c_im.astype(f32)
    kmat = (jnp.einsum('ghp,tgpk->tghk', cr, lbr, precision=hp)
            - jnp.einsum('ghp,tgpk->tghk', ci, lbi, precision=hp))
    ng, gw = S5_SG_GROUPS, S5_GROUP
    row_g = jnp.arange(128) // gw
    kt = jnp.swapaxes(kmat[:S5_T], 2, 3).reshape(S5_T, S5_SGS, 128, gw)
    bd = jnp.tile(kt, (1, 1, 1, ng)) * (row_g[:, None] == row_g[None, :])
    bd = jnp.swapaxes(bd, 0, 1)
    st_g = jnp.arange(S5_SG_STATE) // S5_STATE
    wmask = row_g[:, None] == st_g[None, :]

    def w_part(lb):
        x = lb[S5_T - 1::-1][:S5_T].reshape(S5_T, S5_SGS, ng, S5_STATE, gw)
        x = jnp.transpose(x, (1, 0, 2, 4, 3)).reshape(S5_SGS, S5_T, 128, S5_STATE)
        x = jnp.tile(x, (1, 1, 1, ng)) * wmask
        return x.reshape(S5_SGS, S5_LANES, S5_SG_STATE)

    wm = jnp.concatenate([w_part(lbr), w_part(lbi)], axis=2)
    crt, cit = jnp.swapaxes(cr, 1, 2)[:, :, None, :], jnp.swapaxes(ci, 1, 2)[:, :, None, :]
    pr1 = jnp.transpose(pr[1:], (1, 2, 0))[:, :, :, None]
    pi1 = jnp.transpose(pi[1:], (1, 2, 0))[:, :, :, None]
    vr = crt * pr1 - cit * pi1
    vi = crt * pi1 + cit * pr1

    def v_part(v):
        x = v.reshape(S5_SGS, S5_SG_STATE, S5_T, gw)
        x = jnp.tile(x, (1, 1, 1, ng)) * (st_g[:, None, None] == row_g[None, None, :])
        return x.reshape(S5_SGS, S5_SG_STATE, S5_LANES)

    vm = jnp.concatenate([v_part(vr), -v_part(vi)], axis=1)
    mt = bd
    lam_r = pr[S5_T].reshape(S5_SGS, 1, S5_SG_STATE)
    lam_i = pi[S5_T].reshape(S5_SGS, 1, S5_SG_STATE)
    dexp = jnp.tile(d_skip.astype(f32).reshape(S5_SGS, 1, 1, 128), (1, 1, S5_T, 1)).reshape(S5_SGS, 1, S5_LANES)
    return mt.astype(BF16), wm.astype(BF16), vm.astype(BF16), lam_r, lam_i, dexp


def _s5_kernel(u_ref, bd_ref, w_ref, v_ref, ar_ref, ai_ref, d_ref, o_ref,
               zr_ref, zi_ref, sr_ref, si_ref, cr_ref, ci_ref, mt_ref, *, rows):
    @pl.when(pl.program_id(2) == 0)
    def _():
        cr_ref[...] = jnp.zeros(cr_ref.shape, F32)
        ci_ref[...] = jnp.zeros(ci_ref.shape, F32)

    @pl.when(jnp.logical_and(pl.program_id(1) == 0, pl.program_id(2) == 0))
    def _():
        zero = jnp.zeros((128, 128), BF16)
        for s in range(S5_T):
            for t in range(S5_T):
                mt_ref[s * 128:(s + 1) * 128, t * 128:(t + 1) * 128] = bd_ref[t - s] if t >= s else zero

    u = jnp.concatenate([u_ref[pl.ds(t, rows, stride=S5_T), :] for t in range(S5_T)], axis=1)
    ub = u.astype(BF16)

    zz = jnp.dot(ub, w_ref[...], preferred_element_type=F32)
    zr_ref[...] = zz[:, :S5_SG_STATE]
    zi_ref[...] = zz[:, S5_SG_STATE:]

    ar, ai = ar_ref[...], ai_ref[...]

    def step(r, carry):
        sr, si = carry
        sr_ref[pl.ds(r, 1), :] = sr
        si_ref[pl.ds(r, 1), :] = si
        zr = zr_ref[pl.ds(r, 1), :]
        zi = zi_ref[pl.ds(r, 1), :]
        return ar * sr - ai * si + zr, ar * si + ai * sr + zi

    sr, si = lax.fori_loop(0, rows, step, (cr_ref[...], ci_ref[...]))
    cr_ref[...] = sr
    ci_ref[...] = si

    sin = jnp.concatenate([sr_ref[...], si_ref[...]], axis=1).astype(BF16)
    yoff = jnp.dot(sin, v_ref[...], preferred_element_type=F32)
    tile = 256
    for ct in range(S5_LANES // tile):
        cols = slice(ct * tile, (ct + 1) * tile)
        kdim = (ct + 1) * tile
        y = jnp.dot(ub[:, :kdim], mt_ref[:kdim, cols], preferred_element_type=F32)
        y = jax.nn.gelu(y + yoff[:, cols] + u[:, cols] * d_ref[:, cols])
        for t in range(ct * tile // 128, (ct + 1) * tile // 128):
            o_ref[pl.ds(t, rows, stride=S5_T), :] = y[:, t * 128 - ct * tile:(t + 1) * 128 - ct * tile]


def _s5(h, tables, bsz, seq, rows):
    mt, wm, vm, lar, lai, dexp = tables
    n = bsz * seq
    nch = seq // S5_T
    rows = min(rows, nch)
    nt = nch // rows
    uw, uidx = _packed_block('s5_u')
    ublk = uidx * (uw // 128)

    def per_sg(shape):
        return pl.BlockSpec((None,) + shape, lambda a, b, i: (a,) + (0,) * len(shape))

    kern = functools.partial(_s5_kernel, rows=rows)
    return pl.pallas_call(
        kern,
        grid=(S5_SGS, bsz, nt),
        in_specs=[pl.BlockSpec((rows * S5_T, 128), lambda a, b, i: (b * nt + i, ublk + a)),
                  per_sg((S5_T, 128, 128)), per_sg((S5_LANES, 2 * S5_SG_STATE)),
                  per_sg((2 * S5_SG_STATE, S5_LANES)), per_sg((1, S5_SG_STATE)), per_sg((1, S5_SG_STATE)),
                  per_sg((1, S5_LANES))],
        out_specs=pl.BlockSpec((rows * S5_T, 128), lambda a, b, i: (b * nt + i, a)),
        out_shape=jax.ShapeDtypeStruct((n, S5_WIDTH), F32),
        scratch_shapes=[pltpu.VMEM((rows, S5_SG_STATE), F32), pltpu.VMEM((rows, S5_SG_STATE), F32),
                        pltpu.VMEM((rows, S5_SG_STATE), F32), pltpu.VMEM((rows, S5_SG_STATE), F32),
                        pltpu.VMEM((1, S5_SG_STATE), F32), pltpu.VMEM((1, S5_SG_STATE), F32),
                        pltpu.VMEM((S5_LANES, S5_LANES), BF16)],
        compiler_params=_cparams(3),
    )(h, mt, wm, vm, lar, lai, dexp)


def _mem_kernel(q_ref, z_ref, mk_ref, mv_ref, o_ref):
    scale = HEAD_DIM ** -0.5
    mk = mk_ref[...].astype(BF16)
    mv = mv_ref[...].astype(BF16)
    outs = []
    for h in range(MEM_HEADS):
        sl = slice(h * HEAD_DIM, (h + 1) * HEAD_DIM)
        qh = (q_ref[:, sl] * scale).astype(BF16)
        lg = lax.dot_general(qh, mk[:, sl], NT_DIMS, preferred_element_type=F32)
        m = jnp.max(lg, axis=-1, keepdims=True)
        p = jnp.exp(lg - m)
        p = p * (1.0 / jnp.sum(p, axis=-1, keepdims=True))
        outs.append(jnp.dot(p.astype(BF16), mv[:, sl], preferred_element_type=F32))
    o_ref[...] = (jnp.concatenate(outs, axis=1) * _silu(z_ref[...])).astype(o_ref.dtype)


def _mem_attn(h, memkv, bsz, seq, t=512):
    n = bsz * seq
    t = min(t, seq)
    nt = seq // t
    qw, qidx = _packed_block('mem_q')
    zw, zidx = _packed_block('mem_z')
    return pl.pallas_call(
        _mem_kernel,
        grid=(bsz, nt),
        in_specs=[pl.BlockSpec((t, qw), lambda b, i: (b * nt + i, qidx)),
                  pl.BlockSpec((t, zw), lambda b, i: (b * nt + i, zidx)),
                  pl.BlockSpec((MEM_LEN, MEM_WIDTH), lambda b, i: (b, 0)),
                  pl.BlockSpec((MEM_LEN, MEM_WIDTH), lambda b, i: (b, 1))],
        out_specs=pl.BlockSpec((t, MEM_WIDTH), lambda b, i: (b * nt + i, 0)),
        out_shape=jax.ShapeDtypeStruct((n, MEM_WIDTH), BF16),
        compiler_params=_cparams(2),
    )(h, h, memkv, memkv)


def _merge_kernel(x_ref, s5_ref, s5z_ref, att_ref, ssd_ref, mem_ref,
                  wglu_ref, bglu_ref, wg0_ref, wg1_ref, wg2_ref, wg3_ref, bg_ref,
                  ws5_ref, watt_ref, wssd_ref, wmem_ref,
                  wout_ref, lng_ref, lnb_ref, o_ref, xb_ref, ys5_ref, acc_ref):
    nidx = pl.program_id(1)
    wg_ref = (wg0_ref, wg1_ref, wg2_ref, wg3_ref)

    @pl.when(nidx == 0)
    def _():
        xb_ref[...] = x_ref[...].astype(BF16)
        y = s5_ref[...]
        glu = y * _sigmoid(jnp.dot(y.astype(BF16), wglu_ref[...], preferred_element_type=F32) + bglu_ref[...])
        ys5_ref[...] = (glu * _silu(s5z_ref[...])).astype(BF16)
        acc_ref[...] = jnp.zeros(acc_ref.shape, F32)

    xb = xb_ref[...]
    branches = ((ys5_ref, ws5_ref), (att_ref, watt_ref), (ssd_ref, wssd_ref), (mem_ref, wmem_ref))
    merged = None
    for i, (y_ref, w_ref) in enumerate(branches):
        gate = _sigmoid(jnp.dot(xb, wg_ref[i][...], preferred_element_type=F32) + bg_ref[i])
        term = gate * jnp.dot(y_ref[...], w_ref[...], preferred_element_type=F32)
        merged = term if merged is None else merged + term
    acc_ref[...] += jnp.dot(merged.astype(BF16), wout_ref[...], preferred_element_type=F32)

    @pl.when(nidx == pl.num_programs(1) - 1)
    def _():
        r = DEEPNORM_ALPHA * x_ref[...] + acc_ref[...]
        mu = jnp.mean(r, axis=-1, keepdims=True)
        rc = r - mu
        var = jnp.mean(rc * rc, axis=-1, keepdims=True)
        o_ref[...] = rc * lax.rsqrt(var + LN_EPS) * lng_ref[...] + lnb_ref[...]


def _merge(x, h, ys5, yatt, yssd, ymem, wglu, bglu, wg4, bg4, ws5, watt, wssd, wmem, wout, lng, lnb,
           tm=512, tn=256):
    n = x.shape[0]
    tm = min(tm, n)
    zw, zidx = _packed_block('s5_z')
    row = lambda w: pl.BlockSpec((tm, w), lambda i, c: (i, 0))
    colw = lambda k: pl.BlockSpec((k, tn), lambda i, c: (0, c))
    const = lambda shape: pl.BlockSpec(shape, lambda i, c: (0,) * len(shape))
    gate_w = lambda br: pl.BlockSpec((D_MODEL, tn), lambda i, c: (0, br * (D_MODEL // tn) + c))
    return pl.pallas_call(
        _merge_kernel,
        grid=(n // tm, D_MODEL // tn),
        in_specs=[row(D_MODEL), row(S5_WIDTH), pl.BlockSpec((tm, zw), lambda i, c: (i, zidx)),
                  row(ATTN_WIDTH), row(SSD_WIDTH), row(MEM_WIDTH),
                  const((S5_WIDTH, S5_WIDTH)), const((1, S5_WIDTH)),
                  gate_w(0), gate_w(1), gate_w(2), gate_w(3),
                  pl.BlockSpec((N_BRANCH, 1, tn), lambda i, c: (0, 0, c)),
                  colw(S5_WIDTH), colw(ATTN_WIDTH), colw(SSD_WIDTH), colw(MEM_WIDTH),
                  pl.BlockSpec((tn, D_MODEL), lambda i, c: (c, 0)),
                  const((1, D_MODEL)), const((1, D_MODEL))],
        out_specs=pl.BlockSpec((tm, D_MODEL), lambda i, c: (i, 0)),
        out_shape=jax.ShapeDtypeStruct((n, D_MODEL), F32),
        scratch_shapes=[pltpu.VMEM((tm, D_MODEL), BF16), pltpu.VMEM((tm, S5_WIDTH), BF16),
                        pltpu.VMEM((tm, D_MODEL), F32)],
        compiler_params=_cparams(2),
    )(x, ys5, h, yatt, yssd, ymem, wglu, bglu, wg4, wg4, wg4, wg4, bg4, ws5, watt, wssd, wmem, wout, lng, lnb)


def _split_offsets():
    offs, off = {}, 0
    for name, width in SPLITS:
        offs[name] = (off, width)
        off += width
    return offs


def _pack_kernel(w_ref, p_ref, g_ref):
    src = _split_offsets()
    rows = w_ref.shape[0]

    def col(name):
        o, w = src[name]
        return w_ref[:, o:o + w]

    off = 0
    for name, width in PACKED:
        if name == 'idx_kw':
            piece = jnp.concatenate([col('idx_k'), col('idx_w'),
                                     jnp.zeros((rows, width - IDX_DIM - IDX_HEADS), F32)], axis=1)
        elif name == 'ssd_dt':
            piece = jnp.concatenate([col('ssd_dt'), jnp.zeros((rows, width - SSD_HEADS), F32)], axis=1)
        elif name == 'pad':
            piece = jnp.zeros((rows, width), F32)
        else:
            piece = col(name)
        p_ref[:, off:off + width] = piece.astype(BF16)
        off += width
    g_ref[...] = col('gates').astype(BF16)


def _split_w_in(w_in, layer, tr=128):
    _, d, win = w_in.shape
    gw = N_BRANCH * D_MODEL
    return pl.pallas_call(
        _pack_kernel,
        grid=(d // tr,),
        in_specs=[pl.BlockSpec((None, tr, win), lambda r: (layer, r, 0))],
        out_specs=[pl.BlockSpec((tr, PACKED_WIDTH), lambda r: (r, 0)),
                   pl.BlockSpec((tr, gw), lambda r: (r, 0))],
        out_shape=[jax.ShapeDtypeStruct((d, PACKED_WIDTH), BF16),
                   jax.ShapeDtypeStruct((d, gw), BF16)],
        compiler_params=_cparams(1),
    )(w_in)


def _rope_tables(positions):
    half = ROPE_DIM // 2
    inv = ROPE_THETA ** (-jnp.arange(half, dtype=F32) * 2.0 / ROPE_DIM)
    ang = positions.astype(F32).reshape(-1)[:, None] * inv
    cos, sin = jnp.cos(ang), jnp.sin(ang)
    n = ang.shape[0]
    ones = jnp.ones((n, HEAD_DIM - ROPE_DIM), F32)
    zeros = jnp.zeros((n, HEAD_DIM - ROPE_DIM), F32)
    zh = jnp.zeros((n, half), F32)
    c64 = jnp.concatenate([cos, cos, ones], axis=1)
    sp64 = jnp.concatenate([zh, sin, zeros], axis=1)
    sm64 = jnp.concatenate([-sin, zh, zeros], axis=1)
    dup = lambda t: jnp.concatenate([t, t], axis=1)
    return dup(c64), dup(sp64), dup(sm64)


def kernel(x, mem, positions, w_in, b_gate, s5_lam_re, s5_lam_im, s5_log_dt, s5_b_re, s5_b_im, s5_c_re, s5_c_im, s5_d, s5_w_glu, s5_b_glu, ssd_conv_w, ssd_conv_b, ssd_dt_bias, ssd_a_log, ssd_d, ssd_norm_g, mem_w_kv, w_br_s5, w_br_attn, w_br_ssd, w_br_mem, w_out, ln_g, ln_b):
    bsz, seq, d = x.shape
    n = bsz * seq
    depth = w_in.shape[0]
    qb = min(256, seq)
    nq = seq // qb
    ctab, sptab, smtab = _rope_tables(positions)
    xf = x.reshape(n, d).astype(F32)
    memf = mem.reshape(bsz * MEM_LEN, d).astype(F32)

    for i in range(depth):
        w_packed, w_gates = _split_w_in(w_in, i)
        h = _matmul(xf, w_packed, 1024, 512)
        memkv = _matmul(memf, mem_w_kv[i].astype(BF16), 512, 512)

        tables = _s5_tables(s5_lam_re[i], s5_lam_im[i], s5_log_dt[i], s5_b_re[i], s5_b_im[i],
                            s5_c_re[i], s5_c_im[i], s5_d[i])
        ys5 = _s5(h, tables, bsz, seq, 256)

        qp, iqp, kr, vt4, ki, wt = _prep(h, ctab, sptab, smtab, bsz, seq, qb)
        yatt = _dsa(h, qp, iqp, wt, ki, kr, vt4, bsz, seq, qb)

        yssd = _ssd(h, ssd_conv_w[i], ssd_conv_b[i], ssd_dt_bias[i], ssd_a_log[i], ssd_d[i],
                    ssd_norm_g[i], bsz, seq)

        ymem = _mem_attn(h, memkv, bsz, seq)

        xf = _merge(xf, h, ys5, yatt, yssd, ymem,
                    s5_w_glu[i].astype(BF16), s5_b_glu[i].astype(F32).reshape(1, S5_WIDTH),
                    w_gates, b_gate[i].astype(F32).reshape(N_BRANCH, 1, D_MODEL),
                    w_br_s5[i].astype(BF16), w_br_attn[i].astype(BF16), w_br_ssd[i].astype(BF16),
                    w_br_mem[i].astype(BF16), w_out[i].astype(BF16),
                    ln_g[i].astype(F32).reshape(1, D_MODEL), ln_b[i].astype(F32).reshape(1, D_MODEL))
    return xf.reshape(bsz, seq, d).astype(x.dtype)
```
